```python
import math
import jax, jax.numpy as jnp
from jax import lax
import numpy as np

D_MODEL = 2048
BATCH = 1
SEQ = 8192
DEPTH = 4
DEC_BATCH = 4
DEC_SEQ = 2048
PAST_LEN = 128

HEAD_DIM = 128
NA_HEADS = D_MODEL // HEAD_DIM
NA_WIN_ROWS = 8
NA_WIN_COLS = 16
GRID_W = 64
DIL_CONFIGS = ((128, 1), (512, 4), (2048, 16))
NB_GROUPS = len(DIL_CONFIGS)
NB_HEADS = D_MODEL // HEAD_DIM
T5_BUCKETS = 32
T5_MAX_DIST = 1024
X_HEADS = 4
N_MEM = 256
D_FF = 4 * D_MODEL
N_A_LAYERS = (DEPTH + 1) // 2
N_B_LAYERS = DEPTH // 2
RMS_EPS = 1e-6
NEG_INF = -1e30
ATTN_SCALE = 1.0 / math.sqrt(HEAD_DIM)

kernel_name = "hybrid_natten_dilated_encoder"


def rmsnorm(x, g):
    x32 = x.astype(jnp.float32)
    y = x32 * lax.rsqrt(jnp.mean(x32 * x32, axis=-1, keepdims=True) + RMS_EPS)
    return (y * g.astype(jnp.float32)).astype(x.dtype)


def t5_bucket(rel):
    nb = T5_BUCKETS // 2
    max_exact = nb // 2
    ret = jnp.where(rel > 0, nb, 0)
    n = jnp.abs(rel)
    n_f = jnp.maximum(n, 1).astype(jnp.float32)
    large = max_exact + (jnp.log(n_f / max_exact) / math.log(T5_MAX_DIST / max_exact)
                         * (nb - max_exact)).astype(jnp.int32)
    large = jnp.minimum(large, nb - 1)
    return ret + jnp.where(n < max_exact, n, large)


def neighborhood_mixer(h, w_qkv, qn, kn, rpb, w_o):
    bn, s, _ = h.shape
    rows = s // GRID_W
    kh = min(NA_WIN_ROWS, rows)
    qkv = (h @ w_qkv).reshape(bn, s, 3, NA_HEADS, HEAD_DIM)
    q = rmsnorm(qkv[:, :, 0], qn)
    k = rmsnorm(qkv[:, :, 1], kn)
    v = qkv[:, :, 2]
    r = jnp.arange(rows)
    rs = jnp.clip(r - kh // 2, 0, rows - kh)
    key_rows = rs[:, None] + jnp.arange(kh)[None, :]
    c = jnp.arange(GRID_W)
    cs = jnp.clip(c - NA_WIN_COLS // 2, 0, GRID_W - NA_WIN_COLS)
    col_ok = (c[None, :] >= cs[:, None]) & (c[None, :] < cs[:, None] + NA_WIN_COLS)
    qg = q.reshape(bn, rows, GRID_W, NA_HEADS, HEAD_DIM)
    kg = k.reshape(bn, rows, GRID_W, NA_HEADS, HEAD_DIM)[:, key_rows].reshape(bn, rows, kh * GRID_W, NA_HEADS, HEAD_DIM)
    vg = v.reshape(bn, rows, GRID_W, NA_HEADS, HEAD_DIM)[:, key_rows].reshape(bn, rows, kh * GRID_W, NA_HEADS, HEAD_DIM)
    logits = jnp.einsum('brqhd,brkhd->brhqk', qg, kg).astype(jnp.float32) * ATTN_SCALE
    dr = key_rows - r[:, None] + (NA_WIN_ROWS - 1)
    dc = jnp.clip(c[None, :] - c[:, None], -(NA_WIN_COLS - 1), NA_WIN_COLS - 1) + (NA_WIN_COLS - 1)
    bias = rpb[:, dr][..., dc]
    bias = bias.transpose(1, 0, 3, 2, 4).reshape(rows, NA_HEADS, GRID_W, kh * GRID_W)
    mask = jnp.tile(col_ok, (1, kh))
    logits = jnp.where(mask, logits + bias.astype(jnp.float32)[None], NEG_INF)
    p = jax.nn.softmax(logits, axis=-1).astype(v.dtype)
    o = jnp.einsum('brhqk,brkhd->brqhd', p, vg).reshape(bn, s, NA_HEADS * HEAD_DIM)
    return o @ w_o


def dilated_group_attn(q, k, v, t5_g, dil, radius):
    bn, s, nh, dh = q.shape
    L = s // dil
    n = bn * dil

    def to_res(t):
        return t.reshape(bn, L, dil, nh, dh).transpose(0, 2, 1, 3, 4).reshape(n, L, nh, dh)

    q, k, v = to_res(q), to_res(k), to_res(v)
    blk = radius
    nb = -(-L // blk)
    lp = nb * blk
    qp = jnp.pad(q, ((0, 0), (0, lp - L), (0, 0), (0, 0))).reshape(n, nb, blk, nh, dh)

    def windows(t):
        tp = jnp.pad(t, ((0, 0), (radius, lp - L + radius), (0, 0), (0, 0))).reshape(n, nb + 2, blk, nh, dh)
        return jnp.concatenate([tp[:, :nb], tp[:, 1:nb + 1], tp[:, 2:nb + 2]], axis=2)

    kw, vw = windows(k), windows(v)
    i = jnp.arange(blk)
    j = jnp.arange(3 * blk)
    dm = j[None, :] - blk - i[:, None]
    key_idx = jnp.arange(nb)[:, None] * blk - radius + j[None, :]
    valid = (jnp.abs(dm) <= radius)[None] & ((key_idx >= 0) & (key_idx < L))[:, None, :]
    bias = t5_g[t5_bucket(dm * dil)].transpose(2, 0, 1).astype(jnp.float32)
    logits = jnp.einsum('nbqhd,nbkhd->nbhqk', qp, kw).astype(jnp.float32) * ATTN_SCALE + bias[None, None]
    logits = jnp.where(valid[None, :, None], logits, NEG_INF)
    m = jnp.max(logits, axis=-1, keepdims=True)
    p = jnp.exp(logits - m)
    den = jnp.sum(p, axis=-1, keepdims=True)
    o = jnp.einsum('nbhqk,nbkhd->nbqhd', (p / den).astype(v.dtype), vw)
    lse = (m + jnp.log(den))[..., 0].transpose(0, 1, 3, 2)
    o = o.reshape(n, lp, nh, dh)[:, :L]
    lse = lse.reshape(n, lp, nh)[:, :L]
    o = o.reshape(bn, dil, L, nh, dh).transpose(0, 2, 1, 3, 4).reshape(bn, s, nh, dh)
    lse = lse.reshape(bn, dil, L, nh).transpose(0, 2, 1, 3).reshape(bn, s, nh)
    return o, lse


def dilated_mixer(h, w_qkv, qn, kn, t5_table, w_o):
    bn, s, _ = h.shape
    qkv = (h @ w_qkv).reshape(bn, s, NB_GROUPS, 3, NB_HEADS, HEAD_DIM)
    outs, lses = [], []
    for g, (window, dil) in enumerate(DIL_CONFIGS):
        q = rmsnorm(qkv[:, :, g, 0], qn[g])
        k = rmsnorm(qkv[:, :, g, 1], kn[g])
        v = qkv[:, :, g, 2]
        o, lse = dilated_group_attn(q, k, v, t5_table[:, g], dil, window // (2 * dil))
        outs.append(o)
        lses.append(lse)
    wts = jax.nn.softmax(jnp.stack(lses, axis=0), axis=0)
    o = jnp.sum(wts[..., None].astype(outs[0].dtype) * jnp.stack(outs, axis=0), axis=0)
    return o.reshape(bn, s, NB_HEADS * HEAD_DIM) @ w_o


def memory_cross_attn(h, mem_n, w_q, w_kv, qn, kn, w_o):
    bn, s, _ = h.shape
    nm = mem_n.shape[1]
    q = rmsnorm((h @ w_q).reshape(bn, s, X_HEADS, HEAD_DIM), qn)
    kv = (mem_n @ w_kv).reshape(bn, nm, 2, X_HEADS, HEAD_DIM)
    k = rmsnorm(kv[:, :, 0], kn)
    v = kv[:, :, 1]
    logits = jnp.einsum('bshd,bmhd->bhsm', q, k).astype(jnp.float32) * ATTN_SCALE
    p = jax.nn.softmax(logits, axis=-1).astype(v.dtype)
    o = jnp.einsum('bhsm,bmhd->bshd', p, v).reshape(bn, s, X_HEADS * HEAD_DIM)
    return o @ w_o


def sqrelu_mlp(h, w_up, w_down):
    a = jax.nn.relu(h @ w_up)
    return (a * a) @ w_down


def trunk(x, mem, g_mix, g_cross, g_mem, g_mlp, w_qkv_a, q_norm_a, k_norm_a, rpb_a, w_o_a,
          w_qkv_b, q_norm_b, k_norm_b, t5_table, w_o_b, w_q_x, w_kv_x, q_norm_x, k_norm_x, w_o_x,
          w_up, w_down):
    for i in range(DEPTH):
        h = rmsnorm(x, g_mix[i])
        li = i // 2
        if i % 2 == 0:
            x = x + neighborhood_mixer(h, w_qkv_a[li], q_norm_a[li], k_norm_a[li], rpb_a[li], w_o_a[li])
        else:
            x = x + dilated_mixer(h, w_qkv_b[li], q_norm_b[li], k_norm_b[li], t5_table, w_o_b[li])
        h = rmsnorm(x, g_cross[i])
        m = rmsnorm(mem, g_mem[i])
        x = x + memory_cross_attn(h, m, w_q_x[i], w_kv_x[i], q_norm_x[i], k_norm_x[i], w_o_x[i])
        h = rmsnorm(x, g_mlp[i])
        x = x + sqrelu_mlp(h, w_up[i], w_down[i])
    return x


def setup_inputs(seed: int = 0) -> dict:
    key = jax.random.key(seed)
    ks = jax.random.split(key, 32)

    def nrm(k, shape, scale):
        return jax.random.normal(k, shape, jnp.float32) * scale

    def gain(k, shape):
        return 1.0 + nrm(k, shape, 0.05)

    d = D_MODEL
    return {
        "x_prompt": nrm(ks[0], (BATCH, SEQ, d), 1.0),
        "x_sample": nrm(ks[1], (DEC_BATCH, DEC_SEQ, d), 1.0),
        "mem_prompt": nrm(ks[2], (BATCH, N_MEM, d), 1.0),
        "mem_sample": nrm(ks[3], (DEC_BATCH, N_MEM, d), 1.0),
        "g_mix": gain(ks[4], (DEPTH, d)),
        "g_cross": gain(ks[5], (DEPTH, d)),
        "g_mem": gain(ks[6], (DEPTH, d)),
        "g_mlp": gain(ks[7], (DEPTH, d)),
        "w_qkv_a": nrm(ks[8], (N_A_LAYERS, d, 3 * NA_HEADS * HEAD_DIM), d ** -0.5),
        "q_norm_a": gain(ks[9], (N_A_LAYERS, HEAD_DIM)),
        "k_norm_a": gain(ks[10], (N_A_LAYERS, HEAD_DIM)),
        "rpb_a": nrm(ks[11], (N_A_LAYERS, NA_HEADS, 2 * NA_WIN_ROWS - 1, 2 * NA_WIN_COLS - 1), 0.1),
        "w_o_a": nrm(ks[12], (N_A_LAYERS, NA_HEADS * HEAD_DIM, d), (NA_HEADS * HEAD_DIM) ** -0.5),
        "w_qkv_b": nrm(ks[13], (N_B_LAYERS, d, NB_GROUPS * 3 * NB_HEADS * HEAD_DIM), d ** -0.5),
        "q_norm_b": gain(ks[14], (N_B_LAYERS, NB_GROUPS, HEAD_DIM)),
        "k_norm_b": gain(ks[15], (N_B_LAYERS, NB_GROUPS, HEAD_DIM)),
        "t5_table": nrm(ks[16], (T5_BUCKETS, NB_GROUPS, NB_HEADS), 0.1),
        "w_o_b": nrm(ks[17], (N_B_LAYERS, NB_HEADS * HEAD_DIM, d), (NB_HEADS * HEAD_DIM) ** -0.5),
        "w_q_x": nrm(ks[18], (DEPTH, d, X_HEADS * HEAD_DIM), d ** -0.5),
        "w_kv_x": nrm(ks[19], (DEPTH, d, 2 * X_HEADS * HEAD_DIM), d ** -0.5),
        "q_norm_x": gain(ks[20], (DEPTH, HEAD_DIM)),
        "k_norm_x": gain(ks[21], (DEPTH, HEAD_DIM)),
        "w_o_x": nrm(ks[22], (DEPTH, X_HEADS * HEAD_DIM, d), (X_HEADS * HEAD_DIM) ** -0.5),
        "w_up": nrm(ks[23], (DEPTH, d, D_FF), d ** -0.5),
        "w_down": nrm(ks[24], (DEPTH, D_FF, d), D_FF ** -0.5),
    }


def reference(x_prompt, x_sample, mem_prompt, mem_sample, g_mix, g_cross, g_mem, g_mlp,
              w_qkv_a, q_norm_a, k_norm_a, rpb_a, w_o_a, w_qkv_b, q_norm_b, k_norm_b, t5_table, w_o_b,
              w_q_x, w_kv_x, q_norm_x, k_norm_x, w_o_x, w_up, w_down):
    y_prompt = trunk(x_prompt, mem_prompt, g_mix, g_cross, g_mem, g_mlp, w_qkv_a, q_norm_a, k_norm_a,
                     rpb_a, w_o_a, w_qkv_b, q_norm_b, k_norm_b, t5_table, w_o_b, w_q_x, w_kv_x,
                     q_norm_x, k_norm_x, w_o_x, w_up, w_down)
    y_sample = trunk(x_sample, mem_sample, g_mix, g_cross, g_mem, g_mlp, w_qkv_a, q_norm_a, k_norm_a,
                     rpb_a, w_o_a, w_qkv_b, q_norm_b, k_norm_b, t5_table, w_o_b, w_q_x, w_kv_x,
                     q_norm_x, k_norm_x, w_o_x, w_up, w_down)
    return (y_prompt, y_sample)
```

```python
import functools
import math

import numpy as np
import jax
import jax.numpy as jnp
from jax import lax
from jax.experimental import pallas as pl
from jax.experimental.pallas import tpu as pltpu

D_MODEL = 2048
SEQ = 8192
DEPTH = 4
DEC_BATCH = 4
DEC_SEQ = 2048
N_TOK = SEQ + DEC_BATCH * DEC_SEQ
N_SEQS = 1 + DEC_BATCH

HEAD_DIM = 128
N_HEADS = D_MODEL // HEAD_DIM
NA_WIN_ROWS = 8
NA_WIN_COLS = 16
GRID_W = 64
DIL_CONFIGS = ((128, 1), (512, 4), (2048, 16))
N_GROUPS = len(DIL_CONFIGS)
DIL_RADIUS = 64
T5_BUCKETS = 32
T5_MAX_DIST = 1024
X_HEADS = 4
N_MEM = 256
D_FF = 4 * D_MODEL
RMS_EPS = 1e-6
ATTN_SCALE = 1.0 / math.sqrt(HEAD_DIM)
NEG_INF = float("-inf")

F32 = jnp.float32
BF16 = jnp.bfloat16
NT_DIMS = (((1,), (1,)), ((), ()))

VMEM_LIMIT_BYTES = 56 * 1024 * 1024

NA_QROWS = 4
NA_BLK = NA_QROWS * GRID_W
NA_KBLKS = 3
NA_NBP = SEQ // NA_BLK
NA_NBS = DEC_SEQ // NA_BLK

DIL_BQ = 128
DIL_KW = DIL_BQ + 2 * DIL_RADIUS


def _params(*sem):
    return pltpu.CompilerParams(dimension_semantics=sem, vmem_limit_bytes=VMEM_LIMIT_BYTES)


def _norm_rows(x, g):
    ms = jnp.mean(x * x, axis=-1, keepdims=True)
    return x * lax.rsqrt(ms + RMS_EPS) * g


def _norm_matmul_body(x_ref, g_ref, w_ref, hg_ref, o_ref, xn_ref, *, n_norm_blocks, n_blocks, tn):
    j = pl.program_id(1)

    @pl.when(j == 0)
    def _():
        xn_ref[...] = _norm_rows(x_ref[...], g_ref[...]).astype(BF16)

    acc = jnp.dot(xn_ref[...], w_ref[...], preferred_element_type=F32)

    def write_head_normed():
        for s in range(tn // HEAD_DIM):
            sl = slice(s * HEAD_DIM, (s + 1) * HEAD_DIM)
            o_ref[:, sl] = _norm_rows(acc[:, sl], hg_ref[:, sl]).astype(o_ref.dtype)

    def write_plain():
        o_ref[...] = acc.astype(o_ref.dtype)

    if n_norm_blocks == 0:
        write_plain()
    elif n_norm_blocks == n_blocks:
        write_head_normed()
    else:
        pl.when(j < n_norm_blocks)(write_head_normed)
        pl.when(j >= n_norm_blocks)(write_plain)


def norm_matmul(x, g, w, head_gain, *, tm, tn, name):
    m, k = x.shape
    n = w.shape[1]
    n_norm = head_gain.shape[1]
    n_blocks = n // tn
    n_norm_blocks = n_norm // tn
    assert m % tm == 0 and n % tn == 0 and n_norm % tn == 0 and n_norm_blocks >= 1
    body = functools.partial(_norm_matmul_body, n_norm_blocks=n_norm_blocks, n_blocks=n_blocks, tn=tn)
    return pl.pallas_call(
        body,
        grid=(m // tm, n_blocks),
        in_specs=[
            pl.BlockSpec((tm, k), lambda i, j: (i, 0)),
            pl.BlockSpec((1, k), lambda i, j: (0, 0)),
            pl.BlockSpec((k, tn), lambda i, j: (0, j)),
            pl.BlockSpec((1, tn), lambda i, j: (0, jnp.minimum(j, n_norm_blocks - 1))),
        ],
        out_specs=pl.BlockSpec((tm, tn), lambda i, j: (i, j)),
        out_shape=jax.ShapeDtypeStruct((m, n), BF16),
        scratch_shapes=[pltpu.VMEM((tm, k), BF16)],
        compiler_params=_params("parallel", "arbitrary"),
        name=name,
    )(x, g.reshape(1, k), w, head_gain)


def _matmul_residual_body(a_ref, w_ref, r_ref, o_ref):
    o_ref[...] = r_ref[...] + jnp.dot(a_ref[...], w_ref[...], preferred_element_type=F32)


def matmul_residual(a, w, res, *, tm, name):
    m, k = a.shape
    n = w.shape[1]
    assert m % tm == 0
    return pl.pallas_call(
        _matmul_residual_body,
        grid=(m // tm,),
        in_specs=[
            pl.BlockSpec((tm, k), lambda i: (i, 0)),
            pl.BlockSpec((k, n), lambda i: (0, 0)),
            pl.BlockSpec((tm, n), lambda i: (i, 0)),
        ],
        out_specs=pl.BlockSpec((tm, n), lambda i: (i, 0)),
        out_shape=jax.ShapeDtypeStruct((m, n), F32),
        compiler_params=_params("parallel"),
        name=name,
    )(a, w, res)


def _mlp_body(x_ref, g_ref, wu_ref, wd_ref, o_ref, xn_ref):
    @pl.when(pl.program_id(1) == 0)
    def _():
        x = x_ref[...]
        xn_ref[...] = _norm_rows(x, g_ref[...]).astype(BF16)
        o_ref[...] = x

    h = jnp.dot(xn_ref[...], wu_ref[...], preferred_element_type=F32)
    h = jnp.maximum(h, 0.0)
    o_ref[...] += jnp.dot((h * h).astype(BF16), wd_ref[...], preferred_element_type=F32)


def mlp_residual(x, g, w_up, w_down, *, tm, tf, name):
    m, d = x.shape
    ff = w_up.shape[1]
    assert m % tm == 0 and ff % tf == 0
    return pl.pallas_call(
        _mlp_body,
        grid=(m // tm, ff // tf),
        in_specs=[
            pl.BlockSpec((tm, d), lambda i, f: (i, 0)),
            pl.BlockSpec((1, d), lambda i, f: (0, 0)),
            pl.BlockSpec((d, tf), lambda i, f: (0, f)),
            pl.BlockSpec((tf, d), lambda i, f: (f, 0)),
        ],
        out_specs=pl.BlockSpec((tm, d), lambda i, f: (i, 0)),
        out_shape=jax.ShapeDtypeStruct((m, d), F32),
        scratch_shapes=[pltpu.VMEM((tm, d), BF16)],
        compiler_params=_params("parallel", "arbitrary"),
        name=name,
    )(x, g.reshape(1, d), w_up, w_down)


def _cross_body(x_ref, g_ref, wq_ref, qn_ref, k_ref, v_ref, wo_ref, o_ref):
    x = x_ref[...]
    xn = _norm_rows(x, g_ref[...]).astype(BF16)
    q = jnp.dot(xn, wq_ref[...], preferred_element_type=F32)
    heads = []
    for h in range(X_HEADS):
        sl = slice(h * HEAD_DIM, (h + 1) * HEAD_DIM)
        qh = _norm_rows(q[:, sl], qn_ref[...]).astype(BF16)
        s = lax.dot_general(qh, k_ref[:, sl], NT_DIMS, preferred_element_type=F32) * ATTN_SCALE
        m = jnp.max(s, axis=-1, keepdims=True)
        p = jnp.exp(s - m)
        den = jnp.sum(p, axis=-1, keepdims=True)
        oh = jnp.dot(p.astype(BF16), v_ref[:, sl], preferred_element_type=F32) / den
        heads.append(oh.astype(BF16))
    o = jnp.concatenate(heads, axis=-1)
    o_ref[...] = x + jnp.dot(o, wo_ref[...], preferred_element_type=F32)


def cross_attn_residual(x, g, w_q, q_gain, kv, w_o, *, tm, name):
    m, d = x.shape
    dx = X_HEADS * HEAD_DIM
    assert SEQ % tm == 0 and DEC_SEQ % tm == 0
    tiles_p = SEQ // tm
    tiles_s = DEC_SEQ // tm

    def mem_seq(i):
        return jnp.where(i < tiles_p, 0, 1 + (i - tiles_p) // tiles_s)

    return pl.pallas_call(
        _cross_body,
        grid=(m // tm,),
        in_specs=[
            pl.BlockSpec((tm, d), lambda i: (i, 0)),
            pl.BlockSpec((1, d), lambda i: (0, 0)),
            pl.BlockSpec((d, dx), lambda i: (0, 0)),
            pl.BlockSpec((1, HEAD_DIM), lambda i: (0, 0)),
            pl.BlockSpec((N_MEM, dx), lambda i: (mem_seq(i), 0)),
            pl.BlockSpec((N_MEM, dx), lambda i: (mem_seq(i), 1)),
            pl.BlockSpec((dx, d), lambda i: (0, 0)),
        ],
        out_specs=pl.BlockSpec((tm, d), lambda i: (i, 0)),
        out_shape=jax.ShapeDtypeStruct((m, d), F32),
        compiler_params=_params("parallel"),
        name=name,
    )(x, g.reshape(1, d), w_q, q_gain.reshape(1, HEAD_DIM), kv, kv, w_o)


def _na_block_info(i):
    in_prompt = i < NA_NBP
    local = jnp.where(in_prompt, i, (i - NA_NBP) % NA_NBS)
    nblk = jnp.where(in_prompt, NA_NBP, NA_NBS)
    base = jnp.where(in_prompt, 0, NA_NBP + ((i - NA_NBP) // NA_NBS) * NA_NBS)
    kblk = base + jnp.clip(local - 1, 0, nblk - NA_KBLKS)
    placement = jnp.where(local == 0, 0, jnp.where(local == nblk - 1, 2, 1))
    return kblk, placement


def _na_body(q_ref, k0_ref, k1_ref, k2_ref, v0_ref, v1_ref, v2_ref, b_ref, o_ref):
    k_refs = (k0_ref, k1_ref, k2_ref)
    v_refs = (v0_ref, v1_ref, v2_ref)
    for h in range(N_HEADS):
        sl = slice(h * HEAD_DIM, (h + 1) * HEAD_DIM)
        q = q_ref[:, sl]
        logits = []
        for c in range(NA_KBLKS):
            s = lax.dot_general(q, k_refs[c][:, sl], NT_DIMS, preferred_element_type=F32)
            logits.append(s * ATTN_SCALE + b_ref[0, h, :, c * NA_BLK:(c + 1) * NA_BLK])
        m = functools.reduce(jnp.maximum, [jnp.max(s, axis=-1, keepdims=True) for s in logits])
        probs = [jnp.exp(s - m) for s in logits]
        den = functools.reduce(jnp.add, [jnp.sum(p, axis=-1, keepdims=True) for p in probs])
        acc = functools.reduce(jnp.add, [
            jnp.dot(probs[c].astype(BF16), v_refs[c][:, sl], preferred_element_type=F32)
            for c in range(NA_KBLKS)])
        o_ref[:, sl] = (acc / den).astype(o_ref.dtype)


def neighborhood_attention(qkv, bias_table, *, name):
    d = D_MODEL

    def kv_spec(c, part):
        return pl.BlockSpec((NA_BLK, d), lambda i: (_na_block_info(i)[0] + c, part))

    return pl.pallas_call(
        _na_body,
        grid=(N_TOK // NA_BLK,),
        in_specs=[pl.BlockSpec((NA_BLK, d), lambda i: (i, 0))]
        + [kv_spec(c, 1) for c in range(NA_KBLKS)]
        + [kv_spec(c, 2) for c in range(NA_KBLKS)]
        + [pl.BlockSpec((1, N_HEADS, NA_BLK, NA_KBLKS * NA_BLK),
                        lambda i: (_na_block_info(i)[1], 0, 0, 0))],
        out_specs=pl.BlockSpec((NA_BLK, d), lambda i: (i, 0)),
        out_shape=jax.ShapeDtypeStruct((N_TOK, d), BF16),
        compiler_params=_params("parallel"),
        name=name,
    )(*([qkv] * (1 + 2 * NA_KBLKS)), bias_table)


def na_bias_table(rpb):
    rq = np.arange(NA_QROWS)[:, None]
    rk = np.arange(NA_KBLKS * NA_QROWS)[None, :]
    full = (NA_QROWS, NA_KBLKS * NA_QROWS)
    dr = np.stack([rk - rq + 7, rk - rq + 3, rk - rq - 1])
    row_ok = np.stack([np.broadcast_to(rk <= 7, full),
                       (rk >= rq) & (rk <= rq + 7),
                       np.broadcast_to(rk >= 4, full)])
    c = np.arange(GRID_W)
    dc = np.clip(c[None, :] - c[:, None], -(NA_WIN_COLS - 1), NA_WIN_COLS - 1) + (NA_WIN_COLS - 1)
    cs = np.clip(c - NA_WIN_COLS // 2, 0, GRID_W - NA_WIN_COLS)
    col_ok = (c[None, :] >= cs[:, None]) & (c[None, :] < cs[:, None] + NA_WIN_COLS)
    ok = row_ok[:, :, :, None, None] & col_ok[None, None, None]
    t = rpb.astype(F32)[:, np.clip(dr, 0, 2 * NA_WIN_ROWS - 2)][..., dc]
    t = jnp.where(ok[None], t, NEG_INF)
    t = t.transpose(1, 0, 2, 4, 3, 5)
    return t.reshape(3, N_HEADS, NA_BLK, NA_KBLKS * NA_BLK)


def _dil_body(*refs, group, dil, tl):
    q_ref, k_ref, v_ref, kp_ref, kn_ref, vp_ref, vn_ref, b_ref = refs[:8]
    refs = refs[8:]
    if group > 0:
        oin_ref, sin_ref = refs[:2]
        refs = refs[2:]
    if group < N_GROUPS - 1:
        oout_ref, sout_ref, kext_ref, vext_ref = refs
        sout_ref[...] = jnp.zeros_like(sout_ref)
    else:
        oout_ref, kext_ref, vext_ref = refs

    row0 = pl.program_id(1) * tl
    rows_prompt = SEQ // dil
    len_sample = DEC_SEQ // dil
    in_prompt = row0 < rows_prompt
    pos0 = jnp.where(in_prompt, row0, (row0 - rows_prompt) % len_sample)
    seq_len = jnp.where(in_prompt, rows_prompt, len_sample)

    for ext_ref, prev_ref, cur_ref, next_ref in ((kext_ref, kp_ref, k_ref, kn_ref),
                                                 (vext_ref, vp_ref, v_ref, vn_ref)):
        ext_ref[0:DIL_RADIUS] = prev_ref[...]
        ext_ref[DIL_RADIUS:DIL_RADIUS + tl] = cur_ref[...]
        ext_ref[DIL_RADIUS + tl:] = next_ref[...]

    def block(i, carry):
        qs = pl.multiple_of(i * DIL_BQ, DIL_BQ)
        rows = pl.ds(qs, DIL_BQ)
        win = pl.ds(qs, DIL_KW)
        kpos = pos0 + qs - DIL_RADIUS + lax.broadcasted_iota(jnp.int32, (1, DIL_KW), 1)
        edge = jnp.where((kpos >= 0) & (kpos < seq_len), 0.0, NEG_INF).astype(F32)
        for h in range(N_HEADS):
            sl = slice(h * HEAD_DIM, (h + 1) * HEAD_DIM)
            s = lax.dot_general(q_ref[rows, sl], kext_ref[win, sl], NT_DIMS,
                                preferred_element_type=F32)
            s = s * ATTN_SCALE + b_ref[0, h] + edge
            m = jnp.max(s, axis=-1, keepdims=True)
            p = jnp.exp(s - m)
            den = jnp.sum(p, axis=-1, keepdims=True)
            o = jnp.dot(p.astype(BF16), vext_ref[win, sl], preferred_element_type=F32) / den
            lse = m + jnp.log(den)
            if group > 0:
                lse_run = sin_ref[rows, h:h + 1]
                top = jnp.maximum(lse_run, lse)
                w_run = jnp.exp(lse_run - top)
                w_new = jnp.exp(lse - top)
                z = w_run + w_new
                o = (oin_ref[rows, sl] * w_run + o * w_new) / z
                lse = top + jnp.log(z)
            oout_ref[rows, sl] = o.astype(oout_ref.dtype)
            if group < N_GROUPS - 1:
                sout_ref[rows, h:h + 1] = lse
        return carry

    lax.fori_loop(0, tl // DIL_BQ, block, 0)


def dilated_group_attention(qkv, bias_table, state, *, group, name):
    d = D_MODEL
    dil = DIL_CONFIGS[group][1]
    assert DIL_CONFIGS[group][0] // (2 * dil) == DIL_RADIUS
    rows = N_TOK // dil
    tl = min(512, DEC_SEQ // dil)
    assert (SEQ // dil) % tl == 0 and (DEC_SEQ // dil) % tl == 0 and tl % DIL_BQ == 0
    n_parts = 3 * N_GROUPS
    halo_per_tile = tl // DIL_RADIUS
    n_halo = rows // DIL_RADIUS
    q_col, k_col, v_col = 2 * group, 2 * group + 1, 2 * N_GROUPS + group

    def main_spec(col):
        return pl.BlockSpec((tl, d), lambda r, t: (t, r * n_parts + col))

    def prev_spec(col):
        return pl.BlockSpec((DIL_RADIUS, d),
                            lambda r, t: (jnp.maximum(t * halo_per_tile - 1, 0), r * n_parts + col))

    def next_spec(col):
        return pl.BlockSpec((DIL_RADIUS, d),
                            lambda r, t: (jnp.minimum((t + 1) * halo_per_tile, n_halo - 1),
                                          r * n_parts + col))

    o_spec = pl.BlockSpec((tl, d), lambda r, t: (t, r))
    s_spec = pl.BlockSpec((tl, HEAD_DIM), lambda r, t: (t, r))
    qkv_v = qkv.reshape(rows, dil * n_parts * d)

    in_specs = [main_spec(q_col), main_spec(k_col), main_spec(v_col),
                prev_spec(k_col), next_spec(k_col), prev_spec(v_col), next_spec(v_col),
                pl.BlockSpec((1, N_HEADS, DIL_BQ, DIL_KW), lambda r, t: (group, 0, 0, 0))]
    args = [qkv_v] * 7 + [bias_table]
    if group > 0:
        in_specs += [o_spec, s_spec]
        args += [state[0].reshape(rows, dil * d), state[1].reshape(rows, dil * HEAD_DIM)]
    if group < N_GROUPS - 1:
        out_specs = [o_spec, s_spec]
        out_shape = [jax.ShapeDtypeStruct((rows, dil * d), F32),
                     jax.ShapeDtypeStruct((rows, dil * HEAD_DIM), F32)]
    else:
        out_specs = o_spec
        out_shape = jax.ShapeDtypeStruct((rows, dil * d), BF16)

    out = pl.pallas_call(
        functools.partial(_dil_body, group=group, dil=dil, tl=tl),
        grid=(dil, rows // tl),
        in_specs=in_specs,
        out_specs=out_specs,
        out_shape=out_shape,
        scratch_shapes=[pltpu.VMEM((tl + 2 * DIL_RADIUS, d), BF16),
                        pltpu.VMEM((tl + 2 * DIL_RADIUS, d), BF16)],
        compiler_params=_params("parallel", "parallel"),
        name=name,
    )(*args)
    if group < N_GROUPS - 1:
        return out[0].reshape(N_TOK, d), out[1].reshape(N_TOK, HEAD_DIM)
    return out.reshape(N_TOK, d)


def _t5_bucket(rel):
    nb = T5_BUCKETS // 2
    max_exact = nb // 2
    ret = jnp.where(rel > 0, nb, 0)
    n = jnp.abs(rel)
    n_f = jnp.maximum(n, 1).astype(F32)
    large = max_exact + (jnp.log(n_f / max_exact) / math.log(T5_MAX_DIST / max_exact)
                         * (nb - max_exact)).astype(jnp.int32)
    large = jnp.minimum(large, nb - 1)
    return ret + jnp.where(n < max_exact, n, large)


def dilated_bias_table(t5_table):
    rel = np.arange(DIL_KW)[None, :] - DIL_RADIUS - np.arange(DIL_BQ)[:, None]
    in_band = (np.abs(rel) <= DIL_RADIUS)[None]
    tables = []
    for g, (_, dil) in enumerate(DIL_CONFIGS):
        b = t5_table[:, g].astype(F32)[_t5_bucket(jnp.asarray(rel * dil, jnp.int32))]
        tables.append(jnp.where(in_band, b.transpose(2, 0, 1), NEG_INF))
    return jnp.stack(tables)


def _tile_heads(gain):
    return jnp.tile(gain.astype(F32), N_HEADS)


def kernel(x_prompt, x_sample, mem_prompt, mem_sample, g_mix, g_cross, g_mem, g_mlp, w_qkv_a, q_norm_a, k_norm_a, rpb_a, w_o_a, w_qkv_b, q_norm_b, k_norm_b, t5_table, w_o_b, w_q_x, w_kv_x, q_norm_x, k_norm_x, w_o_x, w_up, w_down):
    d = D_MODEL
    x = jnp.concatenate([x_prompt.reshape(SEQ, d), x_sample.reshape(DEC_BATCH * DEC_SEQ, d)])
    mem = jnp.concatenate([mem_prompt.reshape(N_MEM, d), mem_sample.reshape(DEC_BATCH * N_MEM, d)])
    dil_bias = dilated_bias_table(t5_table)

    for i in range(DEPTH):
        li = i // 2
        if i % 2 == 0:
            w_qkv = w_qkv_a[li].astype(BF16)
            head_gain = jnp.concatenate([_tile_heads(q_norm_a[li]), _tile_heads(k_norm_a[li])])
            qkv = norm_matmul(x, g_mix[i], w_qkv, head_gain.reshape(1, 2 * d),
                              tm=1024, tn=1024, name=f"qkv_a{i}")
            o = neighborhood_attention(qkv, na_bias_table(rpb_a[li]), name=f"na_attn{i}")
            x = matmul_residual(o, w_o_a[li].astype(BF16), x, tm=512, name=f"wo_a{i}")
        else:
            w = w_qkv_b[li].reshape(d, N_GROUPS, 3, d)
            w_qkv = jnp.concatenate([w[:, :, :2].reshape(d, 2 * N_GROUPS * d),
                                     w[:, :, 2].reshape(d, N_GROUPS * d)], axis=1).astype(BF16)
            head_gain = jnp.concatenate(
                [_tile_heads(gn[li, g]) for g in range(N_GROUPS) for gn in (q_norm_b, k_norm_b)])
            qkv = norm_matmul(x, g_mix[i], w_qkv, head_gain.reshape(1, 2 * N_GROUPS * d),
                              tm=1024, tn=1024, name=f"qkv_b{i}")
            state = None
            for g in range(N_GROUPS):
                state = dilated_group_attention(qkv, dil_bias, state, group=g, name=f"dil_attn{i}_{g}")
            x = matmul_residual(state, w_o_b[li].astype(BF16), x, tm=512, name=f"wo_b{i}")

        kv = norm_matmul(mem, g_mem[i], w_kv_x[i].astype(BF16),
                         _tile_heads(k_norm_x[i])[:X_HEADS * HEAD_DIM].reshape(1, X_HEADS * HEAD_DIM),
                         tm=N_SEQS * N_MEM, tn=X_HEADS * HEAD_DIM, name=f"kv_x{i}")
        x = cross_attn_residual(x, g_cross[i], w_q_x[i].astype(BF16), q_norm_x[i], kv,
                                w_o_x[i].astype(BF16), tm=512, name=f"cross{i}")
        x = mlp_residual(x, g_mlp[i], w_up[i].astype(BF16), w_down[i].astype(BF16),
                         tm=512, tf=1024, name=f"mlp{i}")

    y_prompt = x[:SEQ].reshape(1, SEQ, d)
    y_sample = x[SEQ:].reshape(DEC_BATCH, DEC_SEQ, d)
    return (y_prompt, y_sample)
```

```python
import functools
import math

import numpy as np
import jax
import jax.numpy as jnp
from jax import lax
from jax.experimental import pallas as pl
from jax.experimental.pallas import tpu as pltpu

D_MODEL = 2048
SEQ = 8192
DEPTH = 4
DEC_BATCH = 4
DEC_SEQ = 2048
N_TOK = SEQ + DEC_BATCH * DEC_SEQ
N_SEQS = 1 + DEC_BATCH

HEAD_DIM = 128
N_HEADS = D_MODEL // HEAD_DIM
NA_WIN_ROWS = 8
NA_WIN_COLS = 16
GRID_W = 64
DIL_CONFIGS = ((128, 1), (512, 4), (2048, 16))
N_GROUPS = len(DIL_CONFIGS)
DIL_RADIUS = 64
T5_BUCKETS = 32
T5_MAX_DIST = 1024
X_HEADS = 4
N_MEM = 256
D_FF = 4 * D_MODEL
RMS_EPS = 1e-6
ATTN_SCALE = 1.0 / math.sqrt(HEAD_DIM)
NEG_INF = float("-inf")

F32 = jnp.float32
BF16 = jnp.bfloat16
NT_DIMS = (((1,), (1,)), ((), ()))

VMEM_LIMIT_BYTES = 56 * 1024 * 1024

NA_QROWS = 4
NA_BLK = NA_QROWS * GRID_W
NA_KBLKS = 3
NA_NBP = SEQ // NA_BLK
NA_NBS = DEC_SEQ // NA_BLK

DIL_BQ = 128
DIL_KW = DIL_BQ + 2 * DIL_RADIUS

MATMUL_CHUNK = 256


def _params(*sem):
    return pltpu.CompilerParams(dimension_semantics=sem, vmem_limit_bytes=VMEM_LIMIT_BYTES)


def _norm_rows(x, g):
    ms = jnp.mean(x * x, axis=-1, keepdims=True)
    return x * lax.rsqrt(ms + RMS_EPS) * g


def _head_slice(h):
    return slice(h * HEAD_DIM, (h + 1) * HEAD_DIM)


def _norm_matmul_body(x_ref, g_ref, w_ref, hg_ref, o_ref, xn_ref, *slab_refs,
                      n_norm_blocks, n_blocks, tm, tn, dil):
    j = pl.program_id(1)

    @pl.when(j == 0)
    def _():
        xn_ref[...] = _norm_rows(x_ref[...], g_ref[...]).astype(BF16)

    def run(head_norm):
        for c in range(tn // MATMUL_CHUNK):
            acc = jnp.dot(xn_ref[...], w_ref[:, c * MATMUL_CHUNK:(c + 1) * MATMUL_CHUNK],
                          preferred_element_type=F32)
            for s in range(MATMUL_CHUNK // HEAD_DIM):
                slab = c * (MATMUL_CHUNK // HEAD_DIM) + s
                sl = _head_slice(slab)
                piece = acc[:, _head_slice(s)]
                if head_norm:
                    piece = _norm_rows(piece, hg_ref[:, sl])
                if dil == 1:
                    o_ref[0, :, sl] = piece.astype(o_ref.dtype)
                else:
                    slab_ref = slab_refs[0]
                    slab_ref[slab] = piece
                    for r in range(dil):
                        rows = slab_ref[slab, pl.ds(r, tm // dil, stride=dil), :]
                        o_ref[r, :, sl] = rows.astype(o_ref.dtype)

    if n_norm_blocks == n_blocks:
        run(True)
    else:
        pl.when(j < n_norm_blocks)(functools.partial(run, True))
        pl.when(j >= n_norm_blocks)(functools.partial(run, False))


def norm_matmul(x, g, w, head_gain, *, tm, tn, name, dil=1):
    m, k = x.shape
    n = w.shape[1]
    n_norm = head_gain.shape[1]
    n_blocks = n // tn
    n_norm_blocks = n_norm // tn
    assert m % tm == 0 and n % tn == 0 and n_norm % tn == 0 and n_norm_blocks >= 1
    assert tn % MATMUL_CHUNK == 0 and tm % (16 * dil) == 0
    body = functools.partial(_norm_matmul_body, n_norm_blocks=n_norm_blocks, n_blocks=n_blocks,
                             tm=tm, tn=tn, dil=dil)
    scratch = [pltpu.VMEM((tm, k), BF16)]
    if dil > 1:
        scratch.append(pltpu.VMEM((tn // HEAD_DIM, tm, HEAD_DIM), F32))
    return pl.pallas_call(
        body,
        grid=(m // tm, n_blocks),
        in_specs=[
            pl.BlockSpec((tm, k), lambda i, j: (i, 0)),
            pl.BlockSpec((1, k), lambda i, j: (0, 0)),
            pl.BlockSpec((k, tn), lambda i, j: (0, j)),
            pl.BlockSpec((1, tn), lambda i, j: (0, jnp.minimum(j, n_norm_blocks - 1))),
        ],
        out_specs=pl.BlockSpec((dil, tm // dil, tn), lambda i, j: (0, i, j)),
        out_shape=jax.ShapeDtypeStruct((dil, m // dil, n), BF16),
        scratch_shapes=scratch,
        compiler_params=_params("parallel", "arbitrary"),
        name=name,
    )(x, g.reshape(1, k), w, head_gain)


def _matmul_residual_body(a_ref, w_ref, r_ref, o_ref):
    o_ref[...] = r_ref[...] + jnp.dot(a_ref[...], w_ref[...], preferred_element_type=F32)


def matmul_residual(a, w, res, *, tm, name):
    m, k = a.shape
    n = w.shape[1]
    assert m % tm == 0
    return pl.pallas_call(
        _matmul_residual_body,
        grid=(m // tm,),
        in_specs=[
            pl.BlockSpec((tm, k), lambda i: (i, 0)),
            pl.BlockSpec((k, n), lambda i: (0, 0)),
            pl.BlockSpec((tm, n), lambda i: (i, 0)),
        ],
        out_specs=pl.BlockSpec((tm, n), lambda i: (i, 0)),
        out_shape=jax.ShapeDtypeStruct((m, n), F32),
        compiler_params=_params("parallel"),
        name=name,
    )(a, w, res)


def _mlp_body(x_ref, g_ref, wu_ref, wd_ref, o_ref, xn_ref):
    @pl.when(pl.program_id(1) == 0)
    def _():
        x = x_ref[...]
        xn_ref[...] = _norm_rows(x, g_ref[...]).astype(BF16)
        o_ref[...] = x

    h = jnp.dot(xn_ref[...], wu_ref[...], preferred_element_type=F32)
    h = jnp.maximum(h, 0.0)
    o_ref[...] += jnp.dot((h * h).astype(BF16), wd_ref[...], preferred_element_type=F32)


def mlp_residual(x, g, w_up, w_down, *, tm, tf, name):
    m, d = x.shape
    ff = w_up.shape[1]
    assert m % tm == 0 and ff % tf == 0
    return pl.pallas_call(
        _mlp_body,
        grid=(m // tm, ff // tf),
        in_specs=[
            pl.BlockSpec((tm, d), lambda i, f: (i, 0)),
            pl.BlockSpec((1, d), lambda i, f: (0, 0)),
            pl.BlockSpec((d, tf), lambda i, f: (0, f)),
            pl.BlockSpec((tf, d), lambda i, f: (f, 0)),
        ],
        out_specs=pl.BlockSpec((tm, d), lambda i, f: (i, 0)),
        out_shape=jax.ShapeDtypeStruct((m, d), F32),
        scratch_shapes=[pltpu.VMEM((tm, d), BF16)],
        compiler_params=_params("parallel", "arbitrary"),
        name=name,
    )(x, g.reshape(1, d), w_up, w_down)


def _cross_body(x_ref, g_ref, wq_ref, qn_ref, k_ref, v_ref, wo_ref, o_ref):
    x = x_ref[...]
    xn = _norm_rows(x, g_ref[...]).astype(BF16)
    q = jnp.dot(xn, wq_ref[...], preferred_element_type=F32)
    heads = []
    for h in range(X_HEADS):
        sl = _head_slice(h)
        qh = _norm_rows(q[:, sl], qn_ref[...]).astype(BF16)
        s = lax.dot_general(qh, k_ref[:, sl], NT_DIMS, preferred_element_type=F32) * ATTN_SCALE
        m = jnp.max(s, axis=-1, keepdims=True)
        p = jnp.exp(s - m)
        den = jnp.sum(p, axis=-1, keepdims=True)
        oh = jnp.dot(p.astype(BF16), v_ref[:, sl], preferred_element_type=F32) / den
        heads.append(oh.astype(BF16))
    o = jnp.concatenate(heads, axis=-1)
    o_ref[...] = x + jnp.dot(o, wo_ref[...], preferred_element_type=F32)


def cross_attn_residual(x, g, w_q, q_gain, kv, w_o, *, tm, name):
    m, d = x.shape
    dx = X_HEADS * HEAD_DIM
    assert SEQ % tm == 0 and DEC_SEQ % tm == 0
    tiles_p = SEQ // tm
    tiles_s = DEC_SEQ // tm

    def mem_seq(i):
        return jnp.where(i < tiles_p, 0, 1 + (i - tiles_p) // tiles_s)

    return pl.pallas_call(
        _cross_body,
        grid=(m // tm,),
        in_specs=[
            pl.BlockSpec((tm, d), lambda i: (i, 0)),
            pl.BlockSpec((1, d), lambda i: (0, 0)),
            pl.BlockSpec((d, dx), lambda i: (0, 0)),
            pl.BlockSpec((1, HEAD_DIM), lambda i: (0, 0)),
            pl.BlockSpec((N_MEM, dx), lambda i: (mem_seq(i), 0)),
            pl.BlockSpec((N_MEM, dx), lambda i: (mem_seq(i), 1)),
            pl.BlockSpec((dx, d), lambda i: (0, 0)),
        ],
        out_specs=pl.BlockSpec((tm, d), lambda i: (i, 0)),
        out_shape=jax.ShapeDtypeStruct((m, d), F32),
        compiler_params=_params("parallel"),
        name=name,
    )(x, g.reshape(1, d), w_q, q_gain.reshape(1, HEAD_DIM), kv, kv, w_o)


def _na_block_info(i):
    in_prompt = i < NA_NBP
    local = jnp.where(in_prompt, i, (i - NA_NBP) % NA_NBS)
    nblk = jnp.where(in_prompt, NA_NBP, NA_NBS)
    base = jnp.where(in_prompt, 0, NA_NBP + ((i - NA_NBP) // NA_NBS) * NA_NBS)
    kblk = base + jnp.clip(local - 1, 0, nblk - NA_KBLKS)
    placement = jnp.where(local == 0, 0, jnp.where(local == nblk - 1, 2, 1))
    return kblk, placement


def _na_body(q_ref, k0_ref, k1_ref, k2_ref, v0_ref, v1_ref, v2_ref, b_ref, o_ref):
    k_refs = (k0_ref, k1_ref, k2_ref)
    v_refs = (v0_ref, v1_ref, v2_ref)
    for h in range(N_HEADS):
        sl = _head_slice(h)
        q = q_ref[:, sl]
        logits = []
        for c in range(NA_KBLKS):
            s = lax.dot_general(q, k_refs[c][:, sl], NT_DIMS, preferred_element_type=F32)
            logits.append(s * ATTN_SCALE + b_ref[0, h, :, c * NA_BLK:(c + 1) * NA_BLK])
        m = functools.reduce(jnp.maximum, [jnp.max(s, axis=-1, keepdims=True) for s in logits])
        probs = [jnp.exp(s - m) for s in logits]
        den = functools.reduce(jnp.add, [jnp.sum(p, axis=-1, keepdims=True) for p in probs])
        acc = functools.reduce(jnp.add, [
            jnp.dot(probs[c].astype(BF16), v_refs[c][:, sl], preferred_element_type=F32)
            for c in range(NA_KBLKS)])
        o_ref[:, sl] = (acc / den).astype(o_ref.dtype)


def neighborhood_attention(qkv, bias_table, *, name):
    d = D_MODEL

    def kv_spec(c, part):
        return pl.BlockSpec((NA_BLK, d), lambda i: (_na_block_info(i)[0] + c, part))

    return pl.pallas_call(
        _na_body,
        grid=(N_TOK // NA_BLK,),
        in_specs=[pl.BlockSpec((NA_BLK, d), lambda i: (i, 0))]
        + [kv_spec(c, 1) for c in range(NA_KBLKS)]
        + [kv_spec(c, 2) for c in range(NA_KBLKS)]
        + [pl.BlockSpec((1, N_HEADS, NA_BLK, NA_KBLKS * NA_BLK),
                        lambda i: (_na_block_info(i)[1], 0, 0, 0))],
        out_specs=pl.BlockSpec((NA_BLK, d), lambda i: (i, 0)),
        out_shape=jax.ShapeDtypeStruct((N_TOK, d), BF16),
        compiler_params=_params("parallel"),
        name=name,
    )(*([qkv] * (1 + 2 * NA_KBLKS)), bias_table)


def na_bias_table(rpb):
    n_krows = NA_KBLKS * NA_QROWS
    c = np.arange(GRID_W)
    dc = np.clip(c[None, :] - c[:, None], -(NA_WIN_COLS - 1), NA_WIN_COLS - 1) + (NA_WIN_COLS - 1)
    cs = np.clip(c - NA_WIN_COLS // 2, 0, GRID_W - NA_WIN_COLS)
    col_ok = (c[None, :] >= cs[:, None]) & (c[None, :] < cs[:, None] + NA_WIN_COLS)
    tiles = jnp.where(col_ok, rpb.astype(F32)[:, :, dc], NEG_INF)
    masked = jnp.full((N_HEADS, GRID_W, GRID_W), NEG_INF, F32)

    def row_offset(placement, rq, rk):
        if placement == 0:
            return rk - rq + 7 if rk <= 7 else None
        if placement == 1:
            return rk - rq + 3 if rq <= rk <= rq + 7 else None
        return rk - rq - 1 if rk >= 4 else None

    placements = []
    for placement in range(3):
        q_rows = []
        for rq in range(NA_QROWS):
            offs = [row_offset(placement, rq, rk) for rk in range(n_krows)]
            q_rows.append(jnp.concatenate(
                [masked if dr is None else tiles[:, dr] for dr in offs], axis=-1))
        placements.append(jnp.concatenate(q_rows, axis=1))
    return jnp.stack(placements)


def _dil_body(q_ref, k_ref, v_ref, kp_ref, kn_ref, vp_ref, vn_ref, b_ref, o_ref, lse_ref,
              kext_ref, vext_ref, *, dil, tl):
    lse_ref[...] = jnp.zeros_like(lse_ref)

    row0 = pl.program_id(1) * tl
    rows_prompt = SEQ // dil
    len_sample = DEC_SEQ // dil
    in_prompt = row0 < rows_prompt
    pos0 = jnp.where(in_prompt, row0, (row0 - rows_prompt) % len_sample)
    seq_len = jnp.where(in_prompt, rows_prompt, len_sample)

    for ext_ref, prev_ref, cur_ref, next_ref in ((kext_ref, kp_ref, k_ref, kn_ref),
                                                 (vext_ref, vp_ref, v_ref, vn_ref)):
        ext_ref[0:DIL_RADIUS] = prev_ref[...]
        ext_ref[DIL_RADIUS:DIL_RADIUS + tl] = cur_ref[...]
        ext_ref[DIL_RADIUS + tl:] = next_ref[...]

    def block(i, carry):
        qs = pl.multiple_of(i * DIL_BQ, DIL_BQ)
        rows = pl.ds(qs, DIL_BQ)
        win = pl.ds(qs, DIL_KW)
        kpos = pos0 + qs - DIL_RADIUS + lax.broadcasted_iota(jnp.int32, (1, DIL_KW), 1)
        edge = jnp.where((kpos >= 0) & (kpos < seq_len), 0.0, NEG_INF).astype(F32)
        for h in range(N_HEADS):
            sl = _head_slice(h)
            s = lax.dot_general(q_ref[rows, sl], kext_ref[win, sl], NT_DIMS,
                                preferred_element_type=F32)
            s = s * ATTN_SCALE + b_ref[0, h] + edge
            m = jnp.max(s, axis=-1, keepdims=True)
            p = jnp.exp(s - m)
            den = jnp.sum(p, axis=-1, keepdims=True)
            o = jnp.dot(p.astype(BF16), vext_ref[win, sl], preferred_element_type=F32) / den
            o_ref[rows, sl] = o
            lse_ref[rows, h:h + 1] = m + jnp.log(den)
        return carry

    lax.fori_loop(0, tl // DIL_BQ, block, 0)


def dilated_group_attention(qkv, bias_table, *, group, name):
    d = D_MODEL
    dil = DIL_CONFIGS[group][1]
    assert DIL_CONFIGS[group][0] // (2 * dil) == DIL_RADIUS
    rows = N_TOK // dil
    tl = min(512, DEC_SEQ // dil)
    assert (SEQ // dil) % tl == 0 and (DEC_SEQ // dil) % tl == 0 and tl % DIL_BQ == 0
    halo_per_tile = tl // DIL_RADIUS
    n_halo = rows // DIL_RADIUS

    def main_spec(part):
        return pl.BlockSpec((None, tl, d), lambda r, t: (r, t, part))

    def prev_spec(part):
        return pl.BlockSpec((None, DIL_RADIUS, d),
                            lambda r, t: (r, jnp.maximum(t * halo_per_tile - 1, 0), part))

    def next_spec(part):
        return pl.BlockSpec((None, DIL_RADIUS, d),
                            lambda r, t: (r, jnp.minimum((t + 1) * halo_per_tile, n_halo - 1), part))

    return pl.pallas_call(
        functools.partial(_dil_body, dil=dil, tl=tl),
        grid=(dil, rows // tl),
        in_specs=[main_spec(0), main_spec(1), main_spec(2),
                  prev_spec(1), next_spec(1), prev_spec(2), next_spec(2),
                  pl.BlockSpec((1, N_HEADS, DIL_BQ, DIL_KW), lambda r, t: (group, 0, 0, 0))],
        out_specs=[pl.BlockSpec((None, tl, d), lambda r, t: (r, t, 0)),
                   pl.BlockSpec((None, tl, HEAD_DIM), lambda r, t: (r, t, 0))],
        out_shape=[jax.ShapeDtypeStruct((dil, rows, d), F32),
                   jax.ShapeDtypeStruct((dil, rows, HEAD_DIM), F32)],
        scratch_shapes=[pltpu.VMEM((tl + 2 * DIL_RADIUS, d), BF16),
                        pltpu.VMEM((tl + 2 * DIL_RADIUS, d), BF16)],
        compiler_params=_params("parallel", "parallel"),
        name=name,
    )(*([qkv] * 7), bias_table)


def _t5_bucket(rel):
    nb = T5_BUCKETS // 2
    max_exact = nb // 2
    ret = jnp.where(rel > 0, nb, 0)
    n = jnp.abs(rel)
    n_f = jnp.maximum(n, 1).astype(F32)
    large = max_exact + (jnp.log(n_f / max_exact) / math.log(T5_MAX_DIST / max_exact)
                         * (nb - max_exact)).astype(jnp.int32)
    large = jnp.minimum(large, nb - 1)
    return ret + jnp.where(n < max_exact, n, large)


def dilated_bias_table(t5_table):
    rel = np.arange(DIL_KW)[None, :] - DIL_RADIUS - np.arange(DIL_BQ)[:, None]
    in_band = (np.abs(rel) <= DIL_RADIUS)[None]
    tables = []
    for g, (_, dil) in enumerate(DIL_CONFIGS):
        b = t5_table[:, g].astype(F32)[_t5_bucket(jnp.asarray(rel * dil, jnp.int32))]
        tables.append(jnp.where(in_band, b.transpose(2, 0, 1), NEG_INF))
    return jnp.stack(tables)


def _merge_project_body(*refs, tm):
    o_refs = refs[:N_GROUPS]
    l_refs = refs[N_GROUPS:2 * N_GROUPS]
    w_ref, r_ref, out_ref, ltok_ref, otok_ref, a_ref = refs[2 * N_GROUPS:]
    dils = [dil for _, dil in DIL_CONFIGS]
    assert dils[0] == 1

    def to_token_order(dst_ref, idx, src, dil, r):
        if dil == 1:
            dst_ref[idx] = src
        else:
            dst_ref[idx + (pl.ds(r, tm // dil, stride=dil), slice(None))] = src

    for g, dil in enumerate(dils):
        for r in range(dil):
            to_token_order(ltok_ref, (g,), l_refs[g][r], dil, r)
    lses = [ltok_ref[g] for g in range(N_GROUPS)]
    top = functools.reduce(jnp.maximum, lses)
    wts = [jnp.exp(l - top) for l in lses]
    z = functools.reduce(jnp.add, wts)
    wts = [w / z for w in wts]

    for g, dil in enumerate(dils):
        if dil == 1:
            continue
        for r in range(dil):
            for h in range(N_HEADS):
                to_token_order(otok_ref, (g - 1, h), o_refs[g][r, :, _head_slice(h)], dil, r)
    for h in range(N_HEADS):
        sl = _head_slice(h)
        parts = [o_refs[0][0, :, sl]] + [otok_ref[g - 1, h] for g in range(1, N_GROUPS)]
        merged = functools.reduce(jnp.add, [p * wts[g][:, h:h + 1] for g, p in enumerate(parts)])
        a_ref[:, sl] = merged.astype(BF16)
    out_ref[...] = r_ref[...] + jnp.dot(a_ref[...], w_ref[...], preferred_element_type=F32)


def merge_project_residual(outs, lses, w, res, *, tm, name):
    m, d = res.shape
    dils = [dil for _, dil in DIL_CONFIGS]
    assert m % tm == 0 and all(tm % (8 * dil) == 0 for dil in dils)
    in_specs = [pl.BlockSpec((dil, tm // dil, d), lambda i: (0, i, 0)) for dil in dils]
    in_specs += [pl.BlockSpec((dil, tm // dil, HEAD_DIM), lambda i: (0, i, 0)) for dil in dils]
    in_specs += [pl.BlockSpec((d, d), lambda i: (0, 0)), pl.BlockSpec((tm, d), lambda i: (i, 0))]
    return pl.pallas_call(
        functools.partial(_merge_project_body, tm=tm),
        grid=(m // tm,),
        in_specs=in_specs,
        out_specs=pl.BlockSpec((tm, d), lambda i: (i, 0)),
        out_shape=jax.ShapeDtypeStruct((m, d), F32),
        scratch_shapes=[pltpu.VMEM((N_GROUPS, tm, HEAD_DIM), F32),
                        pltpu.VMEM((N_GROUPS - 1, N_HEADS, tm, HEAD_DIM), F32),
                        pltpu.VMEM((tm, d), BF16)],
        compiler_params=_params("parallel"),
        name=name,
    )(*outs, *lses, w, res)


def _tile_heads(gain, n_heads=N_HEADS):
    return jnp.tile(gain.astype(F32), n_heads)


def kernel(x_prompt, x_sample, mem_prompt, mem_sample, g_mix, g_cross, g_mem, g_mlp, w_qkv_a, q_norm_a, k_norm_a, rpb_a, w_o_a, w_qkv_b, q_norm_b, k_norm_b, t5_table, w_o_b, w_q_x, w_kv_x, q_norm_x, k_norm_x, w_o_x, w_up, w_down):
    d = D_MODEL
    x = jnp.concatenate([x_prompt.reshape(SEQ, d), x_sample.reshape(DEC_BATCH * DEC_SEQ, d)])
    mem = jnp.concatenate([mem_prompt.reshape(N_MEM, d), mem_sample.reshape(DEC_BATCH * N_MEM, d)])
    dil_bias = dilated_bias_table(t5_table)

    for i in range(DEPTH):
        li = i // 2
        if i % 2 == 0:
            head_gain = jnp.concatenate([_tile_heads(q_norm_a[li]), _tile_heads(k_norm_a[li])])
            qkv = norm_matmul(x, g_mix[i], w_qkv_a[li].astype(BF16), head_gain.reshape(1, 2 * d),
                              tm=1024, tn=1024, name=f"qkv_a{i}")
            o = neighborhood_attention(qkv.reshape(N_TOK, 3 * d), na_bias_table(rpb_a[li]),
                                       name=f"na_attn{i}")
            x = matmul_residual(o, w_o_a[li].astype(BF16), x, tm=512, name=f"wo_a{i}")
        else:
            outs, lses = [], []
            for g, (_, dil) in enumerate(DIL_CONFIGS):
                w_g = w_qkv_b[li][:, g * 3 * d:(g + 1) * 3 * d].astype(BF16)
                head_gain = jnp.concatenate([_tile_heads(q_norm_b[li, g]), _tile_heads(k_norm_b[li, g])])
                qkv = norm_matmul(x, g_mix[i], w_g, head_gain.reshape(1, 2 * d),
                                  tm=1024, tn=1024, dil=dil, name=f"qkv_b{i}_{g}")
                o_g, lse_g = dilated_group_attention(qkv, dil_bias, group=g, name=f"dil_attn{i}_{g}")
                outs.append(o_g)
                lses.append(lse_g)
            x = merge_project_residual(outs, lses, w_o_b[li].astype(BF16), x, tm=256, name=f"wo_b{i}")

        kv = norm_matmul(mem, g_mem[i], w_kv_x[i].astype(BF16),
                         _tile_heads(k_norm_x[i], X_HEADS).reshape(1, X_HEADS * HEAD_DIM),
                         tm=N_SEQS * N_MEM, tn=X_HEADS * HEAD_DIM, name=f"kv_x{i}")
        x = cross_attn_residual(x, g_cross[i], w_q_x[i].astype(BF16), q_norm_x[i],
                                kv.reshape(N_SEQS * N_MEM, 2 * X_HEADS * HEAD_DIM),
                                w_o_x[i].astype(BF16), tm=512, name=f"cross{i}")
        x = mlp_residual(x, g_mlp[i], w_up[i].astype(BF16), w_down[i].astype(BF16),
                         tm=512, tf=1024, name=f"mlp{i}")

    y_prompt = x[:SEQ].reshape(1, SEQ, d)
    y_sample = x[SEQ:].reshape(DEC_BATCH, DEC_SEQ, d)
    return (y_prompt, y_sample)
```

```python
import functools
import math

import numpy as np
import jax
import jax.numpy as jnp
from jax import lax
from jax.experimental import pallas as pl
from jax.experimental.pallas import tpu as pltpu

D_MODEL = 2048
SEQ = 8192
DEPTH = 4
DEC_BATCH = 4
DEC_SEQ = 2048
N_TOK = SEQ + DEC_BATCH * DEC_SEQ
N_SEQS = 1 + DEC_BATCH

HEAD_DIM = 128
N_HEADS = D_MODEL // HEAD_DIM
NA_WIN_ROWS = 8
NA_WIN_COLS = 16
GRID_W = 64
DIL_CONFIGS = ((128, 1), (512, 4), (2048, 16))
N_GROUPS = len(DIL_CONFIGS)
DIL_RADIUS = 64
T5_BUCKETS = 32
T5_MAX_DIST = 1024
X_HEADS = 4
N_MEM = 256
D_FF = 4 * D_MODEL
RMS_EPS = 1e-6
ATTN_SCALE = 1.0 / math.sqrt(HEAD_DIM)
NEG_INF = float("-inf")

F32 = jnp.float32
BF16 = jnp.bfloat16
NT_DIMS = (((1,), (1,)), ((), ()))

VMEM_LIMIT_BYTES = 56 * 1024 * 1024

NA_QROWS = 4
NA_BLK = NA_QROWS * GRID_W
NA_KBLKS = 3
NA_NBP = SEQ // NA_BLK
NA_NBS = DEC_SEQ // NA_BLK

DIL_BQ = 128
DIL_KW = DIL_BQ + 2 * DIL_RADIUS

MATMUL_CHUNK = 256
DEINTERLEAVE_STRIDE = 4


def _params(*sem):
    return pltpu.CompilerParams(dimension_semantics=sem, vmem_limit_bytes=VMEM_LIMIT_BYTES)


def _norm_rows(x, g):
    ms = jnp.mean(x * x, axis=-1, keepdims=True)
    return x * lax.rsqrt(ms + RMS_EPS) * g


def _head_slice(h):
    return slice(h * HEAD_DIM, (h + 1) * HEAD_DIM)


def _norm_matmul_body(x_ref, g_ref, w_ref, hg_ref, o_ref, xn_ref, *slab_refs,
                      n_norm_blocks, n_blocks, tm, tn, dil):
    j = pl.program_id(1)

    @pl.when(j == 0)
    def _():
        xn_ref[...] = _norm_rows(x_ref[...], g_ref[...]).astype(BF16)

    def run(head_norm):
        for c in range(tn // MATMUL_CHUNK):
            acc = jnp.dot(xn_ref[...], w_ref[:, c * MATMUL_CHUNK:(c + 1) * MATMUL_CHUNK],
                          preferred_element_type=F32)
            for s in range(MATMUL_CHUNK // HEAD_DIM):
                slab = c * (MATMUL_CHUNK // HEAD_DIM) + s
                sl = _head_slice(slab)
                piece = acc[:, _head_slice(s)]
                if head_norm:
                    piece = _norm_rows(piece, hg_ref[:, sl])
                if dil == 1:
                    o_ref[0, :, sl] = piece.astype(o_ref.dtype)
                    continue
                slab_ref = slab_refs[0]
                slab_ref[slab] = piece
                if dil == DEINTERLEAVE_STRIDE:
                    for r in range(dil):
                        rows = slab_ref[slab, pl.ds(r, tm // dil, stride=dil), :]
                        o_ref[r, :, sl] = rows.astype(o_ref.dtype)
                else:
                    quarter_ref = slab_refs[1]
                    outer = dil // DEINTERLEAVE_STRIDE
                    for b in range(DEINTERLEAVE_STRIDE):
                        quarter_ref[slab, b] = slab_ref[
                            slab, pl.ds(b, tm // DEINTERLEAVE_STRIDE, stride=DEINTERLEAVE_STRIDE), :]
                        for a in range(outer):
                            rows = quarter_ref[slab, b, pl.ds(a, tm // dil, stride=outer), :]
                            o_ref[a * DEINTERLEAVE_STRIDE + b, :, sl] = rows.astype(o_ref.dtype)

    if n_norm_blocks == n_blocks:
        run(True)
    else:
        pl.when(j < n_norm_blocks)(functools.partial(run, True))
        pl.when(j >= n_norm_blocks)(functools.partial(run, False))


def norm_matmul(x, g, w, head_gain, *, tm, tn, name, dil=1):
    m, k = x.shape
    n = w.shape[1]
    n_norm = head_gain.shape[1]
    n_blocks = n // tn
    n_norm_blocks = n_norm // tn
    assert m % tm == 0 and n % tn == 0 and n_norm % tn == 0 and n_norm_blocks >= 1
    assert tn % MATMUL_CHUNK == 0 and tm % (16 * dil) == 0
    body = functools.partial(_norm_matmul_body, n_norm_blocks=n_norm_blocks, n_blocks=n_blocks,
                             tm=tm, tn=tn, dil=dil)
    scratch = [pltpu.VMEM((tm, k), BF16)]
    if dil > 1:
        assert dil % DEINTERLEAVE_STRIDE == 0 and dil // DEINTERLEAVE_STRIDE <= DEINTERLEAVE_STRIDE
        scratch.append(pltpu.VMEM((tn // HEAD_DIM, tm, HEAD_DIM), F32))
    if dil > DEINTERLEAVE_STRIDE:
        scratch.append(pltpu.VMEM((tn // HEAD_DIM, DEINTERLEAVE_STRIDE, tm // DEINTERLEAVE_STRIDE,
                                   HEAD_DIM), F32))
    return pl.pallas_call(
        body,
        grid=(m // tm, n_blocks),
        in_specs=[
            pl.BlockSpec((tm, k), lambda i, j: (i, 0)),
            pl.BlockSpec((1, k), lambda i, j: (0, 0)),
            pl.BlockSpec((k, tn), lambda i, j: (0, j)),
            pl.BlockSpec((1, tn), lambda i, j: (0, jnp.minimum(j, n_norm_blocks - 1))),
        ],
        out_specs=pl.BlockSpec((dil, tm // dil, tn), lambda i, j: (0, i, j)),
        out_shape=jax.ShapeDtypeStruct((dil, m // dil, n), BF16),
        scratch_shapes=scratch,
        compiler_params=_params("parallel", "arbitrary"),
        name=name,
    )(x, g.reshape(1, k), w, head_gain)


def _matmul_residual_body(a_ref, w_ref, r_ref, o_ref):
    o_ref[...] = r_ref[...] + jnp.dot(a_ref[...], w_ref[...], preferred_element_type=F32)


def matmul_residual(a, w, res, *, tm, name):
    m, k = a.shape
    n = w.shape[1]
    assert m % tm == 0
    return pl.pallas_call(
        _matmul_residual_body,
        grid=(m // tm,),
        in_specs=[
            pl.BlockSpec((tm, k), lambda i: (i, 0)),
            pl.BlockSpec((k, n), lambda i: (0, 0)),
            pl.BlockSpec((tm, n), lambda i: (i, 0)),
        ],
        out_specs=pl.BlockSpec((tm, n), lambda i: (i, 0)),
        out_shape=jax.ShapeDtypeStruct((m, n), F32),
        compiler_params=_params("parallel"),
        name=name,
    )(a, w, res)


def _mlp_body(x_ref, g_ref, wu_ref, wd_ref, *refs, prompt_tiles):
    xn_ref = refs[-1]

    def step(o_ref):
        @pl.when(pl.program_id(1) == 0)
        def _():
            x = x_ref[...]
            xn_ref[...] = _norm_rows(x, g_ref[...]).astype(BF16)
            o_ref[...] = x

        h = jnp.dot(xn_ref[...], wu_ref[...], preferred_element_type=F32)
        h = jnp.maximum(h, 0.0)
        o_ref[...] += jnp.dot((h * h).astype(BF16), wd_ref[...], preferred_element_type=F32)

    if prompt_tiles is None:
        step(refs[0])
    else:
        i = pl.program_id(0)
        pl.when(i < prompt_tiles)(functools.partial(step, refs[0]))
        pl.when(i >= prompt_tiles)(functools.partial(step, refs[1]))


def mlp_residual(x, g, w_up, w_down, *, tm, tf, name, split_groups=False):
    m, d = x.shape
    ff = w_up.shape[1]
    assert m % tm == 0 and ff % tf == 0 and SEQ % tm == 0
    if split_groups:
        tp = SEQ // tm
        out_specs = [pl.BlockSpec((tm, d), lambda i, f: (jnp.minimum(i, tp - 1), 0)),
                     pl.BlockSpec((tm, d), lambda i, f: (jnp.maximum(i - tp, 0), 0))]
        out_shape = [jax.ShapeDtypeStruct((SEQ, d), F32), jax.ShapeDtypeStruct((m - SEQ, d), F32)]
    else:
        tp = None
        out_specs = pl.BlockSpec((tm, d), lambda i, f: (i, 0))
        out_shape = jax.ShapeDtypeStruct((m, d), F32)
    return pl.pallas_call(
        functools.partial(_mlp_body, prompt_tiles=tp),
        grid=(m // tm, ff // tf),
        in_specs=[
            pl.BlockSpec((tm, d), lambda i, f: (i, 0)),
            pl.BlockSpec((1, d), lambda i, f: (0, 0)),
            pl.BlockSpec((d, tf), lambda i, f: (0, f)),
            pl.BlockSpec((tf, d), lambda i, f: (f, 0)),
        ],
        out_specs=out_specs,
        out_shape=out_shape,
        scratch_shapes=[pltpu.VMEM((tm, d), BF16)],
        compiler_params=_params("arbitrary" if split_groups else "parallel", "arbitrary"),
        name=name,
    )(x, g.reshape(1, d), w_up, w_down)


def _cross_body(x_ref, g_ref, wq_ref, qn_ref, k_ref, v_ref, wo_ref, o_ref):
    x = x_ref[...]
    xn = _norm_rows(x, g_ref[...]).astype(BF16)
    q = jnp.dot(xn, wq_ref[...], preferred_element_type=F32)
    heads = []
    for h in range(X_HEADS):
        sl = _head_slice(h)
        qh = _norm_rows(q[:, sl], qn_ref[...]).astype(BF16)
        s = lax.dot_general(qh, k_ref[:, sl], NT_DIMS, preferred_element_type=F32)
        m = jnp.max(s, axis=-1, keepdims=True)
        p = jnp.exp(s - m)
        den = jnp.sum(p, axis=-1, keepdims=True)
        oh = jnp.dot(p.astype(BF16), v_ref[:, sl], preferred_element_type=F32) / den
        heads.append(oh.astype(BF16))
    o = jnp.concatenate(heads, axis=-1)
    o_ref[...] = x + jnp.dot(o, wo_ref[...], preferred_element_type=F32)


def cross_attn_residual(x, g, w_q, q_gain, kv, w_o, *, tm, name):
    m, d = x.shape
    dx = X_HEADS * HEAD_DIM
    assert SEQ % tm == 0 and DEC_SEQ % tm == 0
    tiles_p = SEQ // tm
    tiles_s = DEC_SEQ // tm

    def mem_seq(i):
        return jnp.where(i < tiles_p, 0, 1 + (i - tiles_p) // tiles_s)

    return pl.pallas_call(
        _cross_body,
        grid=(m // tm,),
        in_specs=[
            pl.BlockSpec((tm, d), lambda i: (i, 0)),
            pl.BlockSpec((1, d), lambda i: (0, 0)),
            pl.BlockSpec((d, dx), lambda i: (0, 0)),
            pl.BlockSpec((1, HEAD_DIM), lambda i: (0, 0)),
            pl.BlockSpec((N_MEM, dx), lambda i: (mem_seq(i), 0)),
            pl.BlockSpec((N_MEM, dx), lambda i: (mem_seq(i), 1)),
            pl.BlockSpec((dx, d), lambda i: (0, 0)),
        ],
        out_specs=pl.BlockSpec((tm, d), lambda i: (i, 0)),
        out_shape=jax.ShapeDtypeStruct((m, d), F32),
        compiler_params=_params("parallel"),
        name=name,
    )(x, g.reshape(1, d), w_q, q_gain.reshape(1, HEAD_DIM), kv, kv, w_o)


def _na_block_info(i):
    in_prompt = i < NA_NBP
    local = jnp.where(in_prompt, i, (i - NA_NBP) % NA_NBS)
    nblk = jnp.where(in_prompt, NA_NBP, NA_NBS)
    base = jnp.where(in_prompt, 0, NA_NBP + ((i - NA_NBP) // NA_NBS) * NA_NBS)
    kblk = base + jnp.clip(local - 1, 0, nblk - NA_KBLKS)
    placement = jnp.where(local == 0, 0, jnp.where(local == nblk - 1, 2, 1))
    return kblk, placement


def _na_body(q_ref, k0_ref, k1_ref, k2_ref, v0_ref, v1_ref, v2_ref, b_ref, o_ref):
    k_refs = (k0_ref, k1_ref, k2_ref)
    v_refs = (v0_ref, v1_ref, v2_ref)
    for h in range(N_HEADS):
        sl = _head_slice(h)
        q = q_ref[:, sl]
        logits = []
        for c in range(NA_KBLKS):
            s = lax.dot_general(q, k_refs[c][:, sl], NT_DIMS, preferred_element_type=F32)
            logits.append(s + b_ref[0, h, :, c * NA_BLK:(c + 1) * NA_BLK])
        m = functools.reduce(jnp.maximum, [jnp.max(s, axis=-1, keepdims=True) for s in logits])
        probs = [jnp.exp(s - m) for s in logits]
        den = functools.reduce(jnp.add, [jnp.sum(p, axis=-1, keepdims=True) for p in probs])
        acc = functools.reduce(jnp.add, [
            jnp.dot(probs[c].astype(BF16), v_refs[c][:, sl], preferred_element_type=F32)
            for c in range(NA_KBLKS)])
        o_ref[:, sl] = (acc / den).astype(o_ref.dtype)


def neighborhood_attention(qkv, bias_table, *, name):
    d = D_MODEL

    def kv_spec(c, part):
        return pl.BlockSpec((NA_BLK, d), lambda i: (_na_block_info(i)[0] + c, part))

    return pl.pallas_call(
        _na_body,
        grid=(N_TOK // NA_BLK,),
        in_specs=[pl.BlockSpec((NA_BLK, d), lambda i: (i, 0))]
        + [kv_spec(c, 1) for c in range(NA_KBLKS)]
        + [kv_spec(c, 2) for c in range(NA_KBLKS)]
        + [pl.BlockSpec((1, N_HEADS, NA_BLK, NA_KBLKS * NA_BLK),
                        lambda i: (_na_block_info(i)[1], 0, 0, 0))],
        out_specs=pl.BlockSpec((NA_BLK, d), lambda i: (i, 0)),
        out_shape=jax.ShapeDtypeStruct((N_TOK, d), BF16),
        compiler_params=_params("parallel"),
        name=name,
    )(*([qkv] * (1 + 2 * NA_KBLKS)), bias_table)


def _toeplitz(v, n_rows, n_cols):
    period = v.shape[-1]
    assert period >= n_rows + n_cols - 1
    flat = jnp.tile(v, (1,) * (v.ndim - 1) + (n_rows,))[..., :n_rows * (period - 1)]
    return flat.reshape(v.shape[:-1] + (n_rows, period - 1))[..., :n_cols]


def na_bias_table(rpb):
    n_krows = NA_KBLKS * NA_QROWS
    c = np.arange(GRID_W)
    cs = np.clip(c - NA_WIN_COLS // 2, 0, GRID_W - NA_WIN_COLS)
    col_ok = (c[None, :] >= cs[:, None]) & (c[None, :] < cs[:, None] + NA_WIN_COLS)
    dist = np.arange(2 * GRID_W)
    dist = np.where(dist < GRID_W, dist, dist - 2 * GRID_W)
    by_dist = rpb.astype(F32)[:, :, np.clip(dist, -(NA_WIN_COLS - 1), NA_WIN_COLS - 1) + (NA_WIN_COLS - 1)]
    tiles = jnp.where(col_ok, _toeplitz(by_dist, GRID_W, GRID_W), NEG_INF)
    masked = jnp.full((N_HEADS, GRID_W, GRID_W), NEG_INF, F32)

    def row_offset(placement, rq, rk):
        if placement == 0:
            return rk - rq + 7 if rk <= 7 else None
        if placement == 1:
            return rk - rq + 3 if rq <= rk <= rq + 7 else None
        return rk - rq - 1 if rk >= 4 else None

    placements = []
    for placement in range(3):
        q_rows = []
        for rq in range(NA_QROWS):
            offs = [row_offset(placement, rq, rk) for rk in range(n_krows)]
            q_rows.append(jnp.concatenate(
                [masked if dr is None else tiles[:, dr] for dr in offs], axis=-1))
        placements.append(jnp.concatenate(q_rows, axis=1))
    return jnp.stack(placements)


def _dil_body(q_ref, k_ref, v_ref, kp_ref, kn_ref, vp_ref, vn_ref, b_ref, o_ref, lse_ref,
              kext_ref, vext_ref, *, dil, tl):
    lse_ref[...] = jnp.zeros_like(lse_ref)

    row0 = pl.program_id(1) * tl
    rows_prompt = SEQ // dil
    len_sample = DEC_SEQ // dil
    in_prompt = row0 < rows_prompt
    pos0 = jnp.where(in_prompt, row0, (row0 - rows_prompt) % len_sample)
    seq_len = jnp.where(in_prompt, rows_prompt, len_sample)

    for ext_ref, prev_ref, cur_ref, next_ref in ((kext_ref, kp_ref, k_ref, kn_ref),
                                                 (vext_ref, vp_ref, v_ref, vn_ref)):
        ext_ref[0:DIL_RADIUS] = prev_ref[...]
        ext_ref[DIL_RADIUS:DIL_RADIUS + tl] = cur_ref[...]
        ext_ref[DIL_RADIUS + tl:] = next_ref[...]

    def block(i, carry):
        qs = pl.multiple_of(i * DIL_BQ, DIL_BQ)
        rows = pl.ds(qs, DIL_BQ)
        win = pl.ds(qs, DIL_KW)
        kpos = pos0 + qs - DIL_RADIUS + lax.broadcasted_iota(jnp.int32, (1, DIL_KW), 1)
        edge = jnp.where((kpos >= 0) & (kpos < seq_len), 0.0, NEG_INF).astype(F32)
        for h in range(N_HEADS):
            sl = _head_slice(h)
            s = lax.dot_general(q_ref[rows, sl], kext_ref[win, sl], NT_DIMS,
                                preferred_element_type=F32)
            s = s + b_ref[0, h] + edge
            m = jnp.max(s, axis=-1, keepdims=True)
            p = jnp.exp(s - m)
            den = jnp.sum(p, axis=-1, keepdims=True)
            o = jnp.dot(p.astype(BF16), vext_ref[win, sl], preferred_element_type=F32) / den
            o_ref[rows, sl] = o
            lse_ref[rows, h:h + 1] = m + jnp.log(den)
        return carry

    lax.fori_loop(0, tl // DIL_BQ, block, 0)


def dilated_group_attention(qkv, bias_table, *, group, name):
    d = D_MODEL
    dil = DIL_CONFIGS[group][1]
    assert DIL_CONFIGS[group][0] // (2 * dil) == DIL_RADIUS
    rows = N_TOK // dil
    tl = min(512, DEC_SEQ // dil)
    assert (SEQ // dil) % tl == 0 and (DEC_SEQ // dil) % tl == 0 and tl % DIL_BQ == 0
    halo_per_tile = tl // DIL_RADIUS
    n_halo = rows // DIL_RADIUS

    def main_spec(part):
        return pl.BlockSpec((None, tl, d), lambda r, t: (r, t, part))

    def prev_spec(part):
        return pl.BlockSpec((None, DIL_RADIUS, d),
                            lambda r, t: (r, jnp.maximum(t * halo_per_tile - 1, 0), part))

    def next_spec(part):
        return pl.BlockSpec((None, DIL_RADIUS, d),
                            lambda r, t: (r, jnp.minimum((t + 1) * halo_per_tile, n_halo - 1), part))

    return pl.pallas_call(
        functools.partial(_dil_body, dil=dil, tl=tl),
        grid=(dil, rows // tl),
        in_specs=[main_spec(0), main_spec(1), main_spec(2),
                  prev_spec(1), next_spec(1), prev_spec(2), next_spec(2),
                  pl.BlockSpec((1, N_HEADS, DIL_BQ, DIL_KW), lambda r, t: (group, 0, 0, 0))],
        out_specs=[pl.BlockSpec((None, tl, d), lambda r, t: (r, t, 0)),
                   pl.BlockSpec((None, tl, HEAD_DIM), lambda r, t: (r, t, 0))],
        out_shape=[jax.ShapeDtypeStruct((dil, rows, d), F32),
                   jax.ShapeDtypeStruct((dil, rows, HEAD_DIM), F32)],
        scratch_shapes=[pltpu.VMEM((tl + 2 * DIL_RADIUS, d), BF16),
                        pltpu.VMEM((tl + 2 * DIL_RADIUS, d), BF16)],
        compiler_params=_params("parallel", "parallel"),
        name=name,
    )(*([qkv] * 7), bias_table)


def _t5_bucket(rel):
    nb = T5_BUCKETS // 2
    max_exact = nb // 2
    ret = jnp.where(rel > 0, nb, 0)
    n = jnp.abs(rel)
    n_f = jnp.maximum(n, 1).astype(F32)
    large = max_exact + (jnp.log(n_f / max_exact) / math.log(T5_MAX_DIST / max_exact)
                         * (nb - max_exact)).astype(jnp.int32)
    large = jnp.minimum(large, nb - 1)
    return ret + jnp.where(n < max_exact, n, large)


def dilated_bias_table(t5_table):
    period = DIL_BQ + DIL_KW
    rel = np.arange(-DIL_RADIUS, DIL_RADIUS + 1)
    pad = jnp.full((N_HEADS, period - rel.size), NEG_INF, F32)
    tables = []
    for g, (_, dil) in enumerate(DIL_CONFIGS):
        band = t5_table[:, g].astype(F32)[_t5_bucket(jnp.asarray(rel * dil, jnp.int32))]
        tables.append(_toeplitz(jnp.concatenate([band.T, pad], axis=-1), DIL_BQ, DIL_KW))
    return jnp.stack(tables)


def _merge_project_body(*refs, tm):
    o_refs = refs[:N_GROUPS]
    l_refs = refs[N_GROUPS:2 * N_GROUPS]
    w_ref, r_ref, out_ref, ltok_ref, otok_ref, a_ref = refs[2 * N_GROUPS:]
    dils = [dil for _, dil in DIL_CONFIGS]
    assert dils[0] == 1

    def to_token_order(dst_ref, idx, src, dil, r):
        if dil == 1:
            dst_ref[idx] = src
        else:
            dst_ref[idx + (pl.ds(r, tm // dil, stride=dil), slice(None))] = src

    for g, dil in enumerate(dils):
        for r in range(dil):
            to_token_order(ltok_ref, (g,), l_refs[g][r], dil, r)
    lses = [ltok_ref[g] for g in range(N_GROUPS)]
    top = functools.reduce(jnp.maximum, lses)
    wts = [jnp.exp(l - top) for l in lses]
    z = functools.reduce(jnp.add, wts)
    wts = [w / z for w in wts]

    for g, dil in enumerate(dils):
        if dil == 1:
            continue
        for r in range(dil):
            for h in range(N_HEADS):
                to_token_order(otok_ref, (g - 1, h), o_refs[g][r, :, _head_slice(h)], dil, r)
    for h in range(N_HEADS):
        sl = _head_slice(h)
        parts = [o_refs[0][0, :, sl]] + [otok_ref[g - 1, h] for g in range(1, N_GROUPS)]
        merged = functools.reduce(jnp.add, [p * wts[g][:, h:h + 1] for g, p in enumerate(parts)])
        a_ref[:, sl] = merged.astype(BF16)
    out_ref[...] = r_ref[...] + jnp.dot(a_ref[...], w_ref[...], preferred_element_type=F32)


def merge_project_residual(outs, lses, w, res, *, tm, name):
    m, d = res.shape
    dils = [dil for _, dil in DIL_CONFIGS]
    assert m % tm == 0 and all(tm % (8 * dil) == 0 for dil in dils)
    in_specs = [pl.BlockSpec((dil, tm // dil, d), lambda i: (0, i, 0)) for dil in dils]
    in_specs += [pl.BlockSpec((dil, tm // dil, HEAD_DIM), lambda i: (0, i, 0)) for dil in dils]
    in_specs += [pl.BlockSpec((d, d), lambda i: (0, 0)), pl.BlockSpec((tm, d), lambda i: (i, 0))]
    return pl.pallas_call(
        functools.partial(_merge_project_body, tm=tm),
        grid=(m // tm,),
        in_specs=in_specs,
        out_specs=pl.BlockSpec((tm, d), lambda i: (i, 0)),
        out_shape=jax.ShapeDtypeStruct((m, d), F32),
        scratch_shapes=[pltpu.VMEM((N_GROUPS, tm, HEAD_DIM), F32),
                        pltpu.VMEM((N_GROUPS - 1, N_HEADS, tm, HEAD_DIM), F32),
                        pltpu.VMEM((tm, d), BF16)],
        compiler_params=_params("parallel"),
        name=name,
    )(*outs, *lses, w, res)


def _tile_heads(gain, n_heads=N_HEADS):
    return jnp.tile(gain.astype(F32), n_heads)


def _qk_gain(q_gain, k_gain):
    return jnp.concatenate([_tile_heads(q_gain) * ATTN_SCALE, _tile_heads(k_gain)]).reshape(1, 2 * D_MODEL)


def kernel(x_prompt, x_sample, mem_prompt, mem_sample, g_mix, g_cross, g_mem, g_mlp, w_qkv_a, q_norm_a, k_norm_a, rpb_a, w_o_a, w_qkv_b, q_norm_b, k_norm_b, t5_table, w_o_b, w_q_x, w_kv_x, q_norm_x, k_norm_x, w_o_x, w_up, w_down):
    d = D_MODEL
    x = jnp.concatenate([x_prompt.reshape(SEQ, d), x_sample.reshape(DEC_BATCH * DEC_SEQ, d)])
    mem = jnp.concatenate([mem_prompt.reshape(N_MEM, d), mem_sample.reshape(DEC_BATCH * N_MEM, d)])
    dil_bias = dilated_bias_table(t5_table)

    for i in range(DEPTH):
        li = i // 2
        if i % 2 == 0:
            qkv = norm_matmul(x, g_mix[i], w_qkv_a[li].astype(BF16),
                              _qk_gain(q_norm_a[li], k_norm_a[li]),
                              tm=1024, tn=1024, name=f"qkv_a{i}")
            o = neighborhood_attention(qkv.reshape(N_TOK, 3 * d), na_bias_table(rpb_a[li]),
                                       name=f"na_attn{i}")
            x = matmul_residual(o, w_o_a[li].astype(BF16), x, tm=512, name=f"wo_a{i}")
        else:
            outs, lses = [], []
            for g, (_, dil) in enumerate(DIL_CONFIGS):
                w_g = w_qkv_b[li][:, g * 3 * d:(g + 1) * 3 * d].astype(BF16)
                qkv = norm_matmul(x, g_mix[i], w_g, _qk_gain(q_norm_b[li, g], k_norm_b[li, g]),
                                  tm=1024, tn=1024, dil=dil, name=f"qkv_b{i}_{g}")
                o_g, lse_g = dilated_group_attention(qkv, dil_bias, group=g, name=f"dil_attn{i}_{g}")
                outs.append(o_g)
                lses.append(lse_g)
            x = merge_project_residual(outs, lses, w_o_b[li].astype(BF16), x, tm=256, name=f"wo_b{i}")

        kv = norm_matmul(mem, g_mem[i], w_kv_x[i].astype(BF16),
                         _tile_heads(k_norm_x[i], X_HEADS).reshape(1, X_HEADS * HEAD_DIM),
                         tm=N_SEQS * N_MEM, tn=X_HEADS * HEAD_DIM, name=f"kv_x{i}")
        x = cross_attn_residual(x, g_cross[i], w_q_x[i].astype(BF16), q_norm_x[i] * ATTN_SCALE,
                                kv.reshape(N_SEQS * N_MEM, 2 * X_HEADS * HEAD_DIM),
                                w_o_x[i].astype(BF16), tm=512, name=f"cross{i}")
        x = mlp_residual(x, g_mlp[i], w_up[i].astype(BF16), w_down[i].astype(BF16),
                         tm=512, tf=1024, split_groups=(i == DEPTH - 1), name=f"mlp{i}")

    y_prompt, y_sample = x
    return (y_prompt.reshape(1, SEQ, d), y_sample.reshape(DEC_BATCH, DEC_SEQ, d))
```

```python
import functools
import math

import numpy as np
import jax
import jax.numpy as jnp
from jax import lax
from jax.experimental import pallas as pl
from jax.experimental.pallas import tpu as pltpu

D_MODEL = 2048
SEQ = 8192
DEPTH = 4
DEC_BATCH = 4
DEC_SEQ = 2048
N_TOK = SEQ + DEC_BATCH * DEC_SEQ
N_SEQS = 1 + DEC_BATCH

HEAD_DIM = 128
N_HEADS = D_MODEL // HEAD_DIM
NA_WIN_ROWS = 8
NA_WIN_COLS = 16
GRID_W = 64
DIL_CONFIGS = ((128, 1), (512, 4), (2048, 16))
N_GROUPS = len(DIL_CONFIGS)
DIL_RADIUS = 64
T5_BUCKETS = 32
T5_MAX_DIST = 1024
X_HEADS = 4
N_MEM = 256
D_FF = 4 * D_MODEL
RMS_EPS = 1e-6
ATTN_SCALE = 1.0 / math.sqrt(HEAD_DIM)
NEG_INF = float("-inf")

F32 = jnp.float32
BF16 = jnp.bfloat16
NT_DIMS = (((1,), (1,)), ((), ()))

VMEM_LIMIT_BYTES = 56 * 1024 * 1024

NA_QROWS = 4
NA_BLK = NA_QROWS * GRID_W
NA_KBLKS = 3
NA_NBP = SEQ // NA_BLK
NA_NBS = DEC_SEQ // NA_BLK

DIL_BQ = 128
DIL_KW = DIL_BQ + 2 * DIL_RADIUS

MATMUL_CHUNK = 256
DEINTERLEAVE_STRIDE = 4


def _params(*sem):
    return pltpu.CompilerParams(dimension_semantics=sem, vmem_limit_bytes=VMEM_LIMIT_BYTES)


def _layer_spec(block, layer, index_map):
    return pl.BlockSpec((None,) + block, lambda *ids: (layer,) + index_map(*ids))


def _group_row_specs(tm, width):
    tp = SEQ // tm
    return tp, [pl.BlockSpec((tm, width), lambda i, *_: (jnp.minimum(i, tp - 1), 0)),
                pl.BlockSpec((tm, width), lambda i, *_: (jnp.maximum(i - tp, 0), 0))]


def _norm_rows(x, g):
    ms = jnp.mean(x * x, axis=-1, keepdims=True)
    return x * lax.rsqrt(ms + RMS_EPS) * g


def _head_slice(h):
    return slice(h * HEAD_DIM, (h + 1) * HEAD_DIM)


def _softmax_pv(logits, values, ones=None):
    m = functools.reduce(jnp.maximum, [jnp.max(s, axis=-1, keepdims=True) for s in logits])
    if ones is None:
        probs = [jnp.exp(s - m) for s in logits]
        den = functools.reduce(jnp.add, [jnp.sum(p, axis=-1, keepdims=True) for p in probs])
        acc = functools.reduce(jnp.add, [jnp.dot(p.astype(BF16), v, preferred_element_type=F32)
                                         for p, v in zip(probs, values)])
        return acc / den, m, den
    res = functools.reduce(jnp.add, [
        jnp.dot(jnp.exp((s - m).astype(BF16)), jnp.concatenate([v, ones], axis=1),
                preferred_element_type=F32)
        for s, v in zip(logits, values)])
    den = res[:, HEAD_DIM:]
    return res[:, :HEAD_DIM] / den, m, den[:, :1]


def _norm_matmul_body(*refs, n_norm_blocks, n_blocks, tm, tn, dil, prompt_tiles):
    n_x = 1 if prompt_tiles is None else 2
    x_refs = refs[:n_x]
    g_ref, w_ref, hg_ref, o_ref, xn_ref = refs[n_x:n_x + 5]
    slab_refs = refs[n_x + 5:]
    i = pl.program_id(0)
    j = pl.program_id(1)

    def normalise(x_ref):
        xn_ref[...] = _norm_rows(x_ref[...], g_ref[...]).astype(BF16)

    if prompt_tiles is None:
        pl.when(j == 0)(functools.partial(normalise, x_refs[0]))
    else:
        pl.when((j == 0) & (i < prompt_tiles))(functools.partial(normalise, x_refs[0]))
        pl.when((j == 0) & (i >= prompt_tiles))(functools.partial(normalise, x_refs[1]))

    def run(head_norm):
        for c in range(tn // MATMUL_CHUNK):
            acc = jnp.dot(xn_ref[...], w_ref[:, c * MATMUL_CHUNK:(c + 1) * MATMUL_CHUNK],
                          preferred_element_type=F32)
            for s in range(MATMUL_CHUNK // HEAD_DIM):
                slab = c * (MATMUL_CHUNK // HEAD_DIM) + s
                sl = _head_slice(slab)
                piece = acc[:, _head_slice(s)]
                if head_norm:
                    piece = _norm_rows(piece, hg_ref[:, sl])
                if dil == 1:
                    o_ref[0, :, sl] = piece.astype(o_ref.dtype)
                    continue
                slab_ref = slab_refs[0]
                slab_ref[slab] = piece
                if dil == DEINTERLEAVE_STRIDE:
                    for r in range(dil):
                        rows = slab_ref[slab, pl.ds(r, tm // dil, stride=dil), :]
                        o_ref[r, :, sl] = rows.astype(o_ref.dtype)
                else:
                    quarter_ref = slab_refs[1]
                    outer = dil // DEINTERLEAVE_STRIDE
                    for b in range(DEINTERLEAVE_STRIDE):
                        quarter_ref[slab, b] = slab_ref[
                            slab, pl.ds(b, tm // DEINTERLEAVE_STRIDE, stride=DEINTERLEAVE_STRIDE), :]
                        for a in range(outer):
                            rows = quarter_ref[slab, b, pl.ds(a, tm // dil, stride=outer), :]
                            o_ref[a * DEINTERLEAVE_STRIDE + b, :, sl] = rows.astype(o_ref.dtype)

    if n_norm_blocks == n_blocks:
        run(True)
    else:
        pl.when(j < n_norm_blocks)(functools.partial(run, True))
        pl.when(j >= n_norm_blocks)(functools.partial(run, False))


def norm_matmul(x, g, w_stack, layer, head_gain, *, n, tm, tn, name, col0=0, dil=1):
    pair = isinstance(x, tuple)
    m = sum(part.shape[0] for part in x) if pair else x.shape[0]
    k = w_stack.shape[1]
    n_norm = head_gain.shape[1]
    n_blocks = n // tn
    n_norm_blocks = n_norm // tn
    assert m % tm == 0 and n % tn == 0 and n_norm % tn == 0 and n_norm_blocks >= 1 and col0 % tn == 0
    assert tn % MATMUL_CHUNK == 0 and tm % (16 * dil) == 0
    if pair:
        prompt_tiles, x_specs = _group_row_specs(tm, k)
        x_args = list(x)
    else:
        prompt_tiles, x_specs, x_args = None, [pl.BlockSpec((tm, k), lambda i, j: (i, 0))], [x]
    body = functools.partial(_norm_matmul_body, n_norm_blocks=n_norm_blocks, n_blocks=n_blocks,
                             tm=tm, tn=tn, dil=dil, prompt_tiles=prompt_tiles)
    scratch = [pltpu.VMEM((tm, k), BF16)]
    if dil > 1:
        assert dil % DEINTERLEAVE_STRIDE == 0 and dil // DEINTERLEAVE_STRIDE <= DEINTERLEAVE_STRIDE
        scratch.append(pltpu.VMEM((tn // HEAD_DIM, tm, HEAD_DIM), F32))
    if dil > DEINTERLEAVE_STRIDE:
        scratch.append(pltpu.VMEM((tn // HEAD_DIM, DEINTERLEAVE_STRIDE, tm // DEINTERLEAVE_STRIDE,
                                   HEAD_DIM), F32))
    return pl.pallas_call(
        body,
        grid=(m // tm, n_blocks),
        in_specs=x_specs + [
            pl.BlockSpec((1, k), lambda i, j: (0, 0)),
            _layer_spec((k, tn), layer, lambda i, j: (0, col0 // tn + j)),
            pl.BlockSpec((1, tn), lambda i, j: (0, jnp.minimum(j, n_norm_blocks - 1))),
        ],
        out_specs=pl.BlockSpec((dil, tm // dil, tn), lambda i, j: (0, i, j)),
        out_shape=jax.ShapeDtypeStruct((dil, m // dil, n), BF16),
        scratch_shapes=scratch,
        compiler_params=_params("arbitrary" if pair else "parallel", "arbitrary"),
        name=name,
    )(*x_args, g.reshape(1, k), w_stack, head_gain)


def _matmul_residual_body(a_ref, w_ref, *refs, prompt_tiles):
    o_ref = refs[-1]
    if prompt_tiles is None:
        res = refs[0][...]
    else:
        res = jnp.where(pl.program_id(0) < prompt_tiles, refs[0][...], refs[1][...])
    o_ref[...] = res + jnp.dot(a_ref[...], w_ref[...], preferred_element_type=F32)


def matmul_residual(a, w_stack, layer, res, *, tm, name):
    m, k = a.shape
    n = w_stack.shape[2]
    assert m % tm == 0
    if isinstance(res, tuple):
        prompt_tiles, res_specs = _group_row_specs(tm, n)
        res_args = list(res)
    else:
        prompt_tiles, res_specs, res_args = None, [pl.BlockSpec((tm, n), lambda i: (i, 0))], [res]
    return pl.pallas_call(
        functools.partial(_matmul_residual_body, prompt_tiles=prompt_tiles),
        grid=(m // tm,),
        in_specs=[pl.BlockSpec((tm, k), lambda i: (i, 0)),
                  _layer_spec((k, n), layer, lambda i: (0, 0))] + res_specs,
        out_specs=pl.BlockSpec((tm, n), lambda i: (i, 0)),
        out_shape=jax.ShapeDtypeStruct((m, n), F32),
        compiler_params=_params("arbitrary" if prompt_tiles is not None else "parallel"),
        name=name,
    )(a, w_stack, *res_args)


def _mlp_body(x_ref, g_ref, wu_ref, wd_ref, *refs, prompt_tiles):
    xn_ref = refs[-1]

    def step(o_ref):
        @pl.when(pl.program_id(1) == 0)
        def _():
            x = x_ref[...]
            xn_ref[...] = _norm_rows(x, g_ref[...]).astype(BF16)
            o_ref[...] = x

        h = jnp.dot(xn_ref[...], wu_ref[...], preferred_element_type=F32)
        h = jnp.maximum(h, 0.0)
        o_ref[...] += jnp.dot((h * h).astype(BF16), wd_ref[...], preferred_element_type=F32)

    if prompt_tiles is None:
        step(refs[0])
    else:
        i = pl.program_id(0)
        pl.when(i < prompt_tiles)(functools.partial(step, refs[0]))
        pl.when(i >= prompt_tiles)(functools.partial(step, refs[1]))


def mlp_residual(x, g, w_up, w_down, layer, *, tm, tf, name, split_groups=False):
    m, d = x.shape
    ff = w_up.shape[2]
    assert m % tm == 0 and ff % tf == 0 and SEQ % tm == 0
    if split_groups:
        tp = SEQ // tm
        out_specs = [pl.BlockSpec((tm, d), lambda i, f: (jnp.minimum(i, tp - 1), 0)),
                     pl.BlockSpec((tm, d), lambda i, f: (jnp.maximum(i - tp, 0), 0))]
        out_shape = [jax.ShapeDtypeStruct((SEQ, d), F32), jax.ShapeDtypeStruct((m - SEQ, d), F32)]
    else:
        tp = None
        out_specs = pl.BlockSpec((tm, d), lambda i, f: (i, 0))
        out_shape = jax.ShapeDtypeStruct((m, d), F32)
    return pl.pallas_call(
        functools.partial(_mlp_body, prompt_tiles=tp),
        grid=(m // tm, ff // tf),
        in_specs=[
            pl.BlockSpec((tm, d), lambda i, f: (i, 0)),
            pl.BlockSpec((1, d), lambda i, f: (0, 0)),
            _layer_spec((d, tf), layer, lambda i, f: (0, f)),
            _layer_spec((tf, d), layer, lambda i, f: (f, 0)),
        ],
        out_specs=out_specs,
        out_shape=out_shape,
        scratch_shapes=[pltpu.VMEM((tm, d), BF16)],
        compiler_params=_params("arbitrary" if split_groups else "parallel", "arbitrary"),
        name=name,
    )(x, g.reshape(1, d), w_up, w_down)


def _cross_body(x_ref, g_ref, wq_ref, qn_ref, k_ref, v_ref, wo_ref, o_ref):
    x = x_ref[...]
    xn = _norm_rows(x, g_ref[...]).astype(BF16)
    q = jnp.dot(xn, wq_ref[...], preferred_element_type=F32)
    heads = []
    for h in range(X_HEADS):
        sl = _head_slice(h)
        qh = _norm_rows(q[:, sl], qn_ref[...]).astype(BF16)
        s = lax.dot_general(qh, k_ref[:, sl], NT_DIMS, preferred_element_type=F32)
        oh, _, _ = _softmax_pv([s], [v_ref[:, sl]])
        heads.append(oh.astype(BF16))
    o = jnp.concatenate(heads, axis=-1)
    o_ref[...] = x + jnp.dot(o, wo_ref[...], preferred_element_type=F32)


def cross_attn_residual(x, g, w_q, q_gain, kv, w_o, layer, *, tm, name):
    m, d = x.shape
    dx = X_HEADS * HEAD_DIM
    assert SEQ % tm == 0 and DEC_SEQ % tm == 0
    tiles_p = SEQ // tm
    tiles_s = DEC_SEQ // tm

    def mem_seq(i):
        return jnp.where(i < tiles_p, 0, 1 + (i - tiles_p) // tiles_s)

    return pl.pallas_call(
        _cross_body,
        grid=(m // tm,),
        in_specs=[
            pl.BlockSpec((tm, d), lambda i: (i, 0)),
            pl.BlockSpec((1, d), lambda i: (0, 0)),
            _layer_spec((d, dx), layer, lambda i: (0, 0)),
            pl.BlockSpec((1, HEAD_DIM), lambda i: (0, 0)),
            pl.BlockSpec((N_MEM, dx), lambda i: (mem_seq(i), 0)),
            pl.BlockSpec((N_MEM, dx), lambda i: (mem_seq(i), 1)),
            _layer_spec((dx, d), layer, lambda i: (0, 0)),
        ],
        out_specs=pl.BlockSpec((tm, d), lambda i: (i, 0)),
        out_shape=jax.ShapeDtypeStruct((m, d), F32),
        compiler_params=_params("parallel"),
        name=name,
    )(x, g.reshape(1, d), w_q, q_gain.reshape(1, HEAD_DIM), kv, kv, w_o)


def _na_block_info(i):
    in_prompt = i < NA_NBP
    local = jnp.where(in_prompt, i, (i - NA_NBP) % NA_NBS)
    nblk = jnp.where(in_prompt, NA_NBP, NA_NBS)
    base = jnp.where(in_prompt, 0, NA_NBP + ((i - NA_NBP) // NA_NBS) * NA_NBS)
    kblk = base + jnp.clip(local - 1, 0, nblk - NA_KBLKS)
    placement = jnp.where(local == 0, 0, jnp.where(local == nblk - 1, 2, 1))
    return kblk, placement


def _na_body(q_ref, k0_ref, k1_ref, k2_ref, v0_ref, v1_ref, v2_ref, b_ref, o_ref):
    k_refs = (k0_ref, k1_ref, k2_ref)
    v_refs = (v0_ref, v1_ref, v2_ref)
    ones = jnp.ones((NA_BLK, HEAD_DIM), BF16)
    for h in range(N_HEADS):
        sl = _head_slice(h)
        q = q_ref[:, sl]
        logits = []
        for c in range(NA_KBLKS):
            s = lax.dot_general(q, k_refs[c][:, sl], NT_DIMS, preferred_element_type=F32)
            logits.append(s + b_ref[0, h, :, c * NA_BLK:(c + 1) * NA_BLK])
        o, _, _ = _softmax_pv(logits, [v_ref[:, sl] for v_ref in v_refs], ones)
        o_ref[:, sl] = o.astype(o_ref.dtype)


def neighborhood_attention(qkv, bias_table, *, name):
    d = D_MODEL

    def kv_spec(c, part):
        return pl.BlockSpec((NA_BLK, d), lambda i: (_na_block_info(i)[0] + c, part))

    return pl.pallas_call(
        _na_body,
        grid=(N_TOK // NA_BLK,),
        in_specs=[pl.BlockSpec((NA_BLK, d), lambda i: (i, 0))]
        + [kv_spec(c, 1) for c in range(NA_KBLKS)]
        + [kv_spec(c, 2) for c in range(NA_KBLKS)]
        + [pl.BlockSpec((1, N_HEADS, NA_BLK, NA_KBLKS * NA_BLK),
                        lambda i: (_na_block_info(i)[1], 0, 0, 0))],
        out_specs=pl.BlockSpec((NA_BLK, d), lambda i: (i, 0)),
        out_shape=jax.ShapeDtypeStruct((N_TOK, d), BF16),
        compiler_params=_params("parallel"),
        name=name,
    )(*([qkv] * (1 + 2 * NA_KBLKS)), bias_table)


def _toeplitz(v, n_rows, n_cols):
    period = v.shape[-1]
    assert period >= n_rows + n_cols - 1
    flat = jnp.tile(v, (1,) * (v.ndim - 1) + (n_rows,))[..., :n_rows * (period - 1)]
    return flat.reshape(v.shape[:-1] + (n_rows, period - 1))[..., :n_cols]


def na_bias_table(rpb):
    n_krows = NA_KBLKS * NA_QROWS
    c = np.arange(GRID_W)
    cs = np.clip(c - NA_WIN_COLS // 2, 0, GRID_W - NA_WIN_COLS)
    col_ok = (c[None, :] >= cs[:, None]) & (c[None, :] < cs[:, None] + NA_WIN_COLS)
    dist = np.arange(2 * GRID_W)
    dist = np.where(dist < GRID_W, dist, dist - 2 * GRID_W)
    by_dist = rpb.astype(F32)[:, :, np.clip(dist, -(NA_WIN_COLS - 1), NA_WIN_COLS - 1) + (NA_WIN_COLS - 1)]
    tiles = jnp.where(col_ok, _toeplitz(by_dist, GRID_W, GRID_W), NEG_INF)
    masked = jnp.full((N_HEADS, GRID_W, GRID_W), NEG_INF, F32)

    def row_offset(placement, rq, rk):
        if placement == 0:
            return rk - rq + 7 if rk <= 7 else None
        if placement == 1:
            return rk - rq + 3 if rq <= rk <= rq + 7 else None
        return rk - rq - 1 if rk >= 4 else None

    placements = []
    for placement in range(3):
        q_rows = []
        for rq in range(NA_QROWS):
            offs = [row_offset(placement, rq, rk) for rk in range(n_krows)]
            q_rows.append(jnp.concatenate(
                [masked if dr is None else tiles[:, dr] for dr in offs], axis=-1))
        placements.append(jnp.concatenate(q_rows, axis=1))
    return jnp.stack(placements)


def _dil_body(q_ref, k_ref, v_ref, kp_ref, kn_ref, vp_ref, vn_ref, b_ref, o_ref, lse_ref,
              kext_ref, vext_ref, *, dil, tl):
    lse_ref[...] = jnp.zeros_like(lse_ref)

    row0 = pl.program_id(1) * tl
    rows_prompt = SEQ // dil
    len_sample = DEC_SEQ // dil
    in_prompt = row0 < rows_prompt
    pos0 = jnp.where(in_prompt, row0, (row0 - rows_prompt) % len_sample)
    seq_len = jnp.where(in_prompt, rows_prompt, len_sample)

    for ext_ref, prev_ref, cur_ref, next_ref in ((kext_ref, kp_ref, k_ref, kn_ref),
                                                 (vext_ref, vp_ref, v_ref, vn_ref)):
        ext_ref[0:DIL_RADIUS] = prev_ref[...]
        ext_ref[DIL_RADIUS:DIL_RADIUS + tl] = cur_ref[...]
        ext_ref[DIL_RADIUS + tl:] = next_ref[...]

    def block(i, carry):
        qs = pl.multiple_of(i * DIL_BQ, DIL_BQ)
        rows = pl.ds(qs, DIL_BQ)
        win = pl.ds(qs, DIL_KW)
        kpos = pos0 + qs - DIL_RADIUS + lax.broadcasted_iota(jnp.int32, (1, DIL_KW), 1)
        edge = jnp.where((kpos >= 0) & (kpos < seq_len), 0.0, NEG_INF).astype(F32)
        for h in range(N_HEADS):
            sl = _head_slice(h)
            s = lax.dot_general(q_ref[rows, sl], kext_ref[win, sl], NT_DIMS,
                                preferred_element_type=F32)
            s = s + b_ref[0, h] + edge
            o, m, den = _softmax_pv([s], [vext_ref[win, sl]])
            o_ref[rows, sl] = o
            lse_ref[rows, h:h + 1] = m + jnp.log(den)
        return carry

    lax.fori_loop(0, tl // DIL_BQ, block, 0)


def dilated_group_attention(qkv, bias_table, *, group, name):
    d = D_MODEL
    dil = DIL_CONFIGS[group][1]
    assert DIL_CONFIGS[group][0] // (2 * dil) == DIL_RADIUS
    rows = N_TOK // dil
    tl = min(512, DEC_SEQ // dil)
    assert (SEQ // dil) % tl == 0 and (DEC_SEQ // dil) % tl == 0 and tl % DIL_BQ == 0
    halo_per_tile = tl // DIL_RADIUS
    n_halo = rows // DIL_RADIUS

    def main_spec(part):
        return pl.BlockSpec((None, tl, d), lambda r, t: (r, t, part))

    def prev_spec(part):
        return pl.BlockSpec((None, DIL_RADIUS, d),
                            lambda r, t: (r, jnp.maximum(t * halo_per_tile - 1, 0), part))

    def next_spec(part):
        return pl.BlockSpec((None, DIL_RADIUS, d),
                            lambda r, t: (r, jnp.minimum((t + 1) * halo_per_tile, n_halo - 1), part))

    return pl.pallas_call(
        functools.partial(_dil_body, dil=dil, tl=tl),
        grid=(dil, rows // tl),
        in_specs=[main_spec(0), main_spec(1), main_spec(2),
                  prev_spec(1), next_spec(1), prev_spec(2), next_spec(2),
                  pl.BlockSpec((1, N_HEADS, DIL_BQ, DIL_KW), lambda r, t: (group, 0, 0, 0))],
        out_specs=[pl.BlockSpec((None, tl, d), lambda r, t: (r, t, 0)),
                   pl.BlockSpec((None, tl, HEAD_DIM), lambda r, t: (r, t, 0))],
        out_shape=[jax.ShapeDtypeStruct((dil, rows, d), F32),
                   jax.ShapeDtypeStruct((dil, rows, HEAD_DIM), F32)],
        scratch_shapes=[pltpu.VMEM((tl + 2 * DIL_RADIUS, d), BF16),
                        pltpu.VMEM((tl + 2 * DIL_RADIUS, d), BF16)],
        compiler_params=_params("parallel", "parallel"),
        name=name,
    )(*([qkv] * 7), bias_table)


def _t5_bucket(rel):
    nb = T5_BUCKETS // 2
    max_exact = nb // 2
    ret = jnp.where(rel > 0, nb, 0)
    n = jnp.abs(rel)
    n_f = jnp.maximum(n, 1).astype(F32)
    large = max_exact + (jnp.log(n_f / max_exact) / math.log(T5_MAX_DIST / max_exact)
                         * (nb - max_exact)).astype(jnp.int32)
    large = jnp.minimum(large, nb - 1)
    return ret + jnp.where(n < max_exact, n, large)


def dilated_bias_table(t5_table):
    period = DIL_BQ + DIL_KW
    rel = np.arange(-DIL_RADIUS, DIL_RADIUS + 1)
    pad = jnp.full((N_HEADS, period - rel.size), NEG_INF, F32)
    tables = []
    for g, (_, dil) in enumerate(DIL_CONFIGS):
        band = t5_table[:, g].astype(F32)[_t5_bucket(jnp.asarray(rel * dil, jnp.int32))]
        tables.append(_toeplitz(jnp.concatenate([band.T, pad], axis=-1), DIL_BQ, DIL_KW))
    return jnp.stack(tables)


def _merge_project_body(*refs, tm):
    o_refs = refs[:N_GROUPS]
    l_refs = refs[N_GROUPS:2 * N_GROUPS]
    w_ref, r_ref, out_ref, ltok_ref, otok_ref, a_ref = refs[2 * N_GROUPS:]
    dils = [dil for _, dil in DIL_CONFIGS]
    assert dils[0] == 1

    def to_token_order(dst_ref, idx, src, dil, r):
        if dil == 1:
            dst_ref[idx] = src
        else:
            dst_ref[idx + (pl.ds(r, tm // dil, stride=dil), slice(None))] = src

    for g, dil in enumerate(dils):
        for r in range(dil):
            to_token_order(ltok_ref, (g,), l_refs[g][r], dil, r)
    lses = [ltok_ref[g] for g in range(N_GROUPS)]
    top = functools.reduce(jnp.maximum, lses)
    wts = [jnp.exp(l - top) for l in lses]
    z = functools.reduce(jnp.add, wts)
    wts = [w / z for w in wts]

    for g, dil in enumerate(dils):
        if dil == 1:
            continue
        for r in range(dil):
            for h in range(N_HEADS):
                to_token_order(otok_ref, (g - 1, h), o_refs[g][r, :, _head_slice(h)], dil, r)
    for h in range(N_HEADS):
        sl = _head_slice(h)
        parts = [o_refs[0][0, :, sl]] + [otok_ref[g - 1, h] for g in range(1, N_GROUPS)]
        merged = functools.reduce(jnp.add, [p * wts[g][:, h:h + 1] for g, p in enumerate(parts)])
        a_ref[:, sl] = merged.astype(BF16)
    out_ref[...] = r_ref[...] + jnp.dot(a_ref[...], w_ref[...], preferred_element_type=F32)


def merge_project_residual(outs, lses, w_stack, layer, res, *, tm, name):
    m, d = res.shape
    dils = [dil for _, dil in DIL_CONFIGS]
    assert m % tm == 0 and all(tm % (8 * dil) == 0 for dil in dils)
    in_specs = [pl.BlockSpec((dil, tm // dil, d), lambda i: (0, i, 0)) for dil in dils]
    in_specs += [pl.BlockSpec((dil, tm // dil, HEAD_DIM), lambda i: (0, i, 0)) for dil in dils]
    in_specs += [_layer_spec((d, d), layer, lambda i: (0, 0)), pl.BlockSpec((tm, d), lambda i: (i, 0))]
    return pl.pallas_call(
        functools.partial(_merge_project_body, tm=tm),
        grid=(m // tm,),
        in_specs=in_specs,
        out_specs=pl.BlockSpec((tm, d), lambda i: (i, 0)),
        out_shape=jax.ShapeDtypeStruct((m, d), F32),
        scratch_shapes=[pltpu.VMEM((N_GROUPS, tm, HEAD_DIM), F32),
                        pltpu.VMEM((N_GROUPS - 1, N_HEADS, tm, HEAD_DIM), F32),
                        pltpu.VMEM((tm, d), BF16)],
        compiler_params=_params("parallel"),
        name=name,
    )(*outs, *lses, w_stack, res)


def _tile_heads(gain, n_heads=N_HEADS):
    return jnp.tile(gain.astype(F32), n_heads)


def _qk_gain(q_gain, k_gain):
    return jnp.concatenate([_tile_heads(q_gain) * ATTN_SCALE, _tile_heads(k_gain)]).reshape(1, 2 * D_MODEL)


def kernel(x_prompt, x_sample, mem_prompt, mem_sample, g_mix, g_cross, g_mem, g_mlp, w_qkv_a, q_norm_a, k_norm_a, rpb_a, w_o_a, w_qkv_b, q_norm_b, k_norm_b, t5_table, w_o_b, w_q_x, w_kv_x, q_norm_x, k_norm_x, w_o_x, w_up, w_down):
    d = D_MODEL
    x = (x_prompt.reshape(SEQ, d), x_sample.reshape(DEC_BATCH * DEC_SEQ, d))
    mem = jnp.concatenate([mem_prompt.reshape(N_MEM, d), mem_sample.reshape(DEC_BATCH * N_MEM, d)])
    dil_bias = dilated_bias_table(t5_table)
    (w_qkv_a, w_o_a, w_qkv_b, w_o_b, w_q_x, w_kv_x, w_o_x, w_up, w_down) = (
        w.astype(BF16) for w in (w_qkv_a, w_o_a, w_qkv_b, w_o_b, w_q_x, w_kv_x, w_o_x, w_up, w_down))

    for i in range(DEPTH):
        li = i // 2
        if i % 2 == 0:
            qkv = norm_matmul(x, g_mix[i], w_qkv_a, li, _qk_gain(q_norm_a[li], k_norm_a[li]),
                              n=3 * d, tm=1024, tn=1024, name=f"qkv_a{i}")
            o = neighborhood_attention(qkv.reshape(N_TOK, 3 * d), na_bias_table(rpb_a[li]),
                                       name=f"na_attn{i}")
            x = matmul_residual(o, w_o_a, li, x, tm=512, name=f"wo_a{i}")
        else:
            outs, lses = [], []
            for g, (_, dil) in enumerate(DIL_CONFIGS):
                qkv = norm_matmul(x, g_mix[i], w_qkv_b, li, _qk_gain(q_norm_b[li, g], k_norm_b[li, g]),
                                  n=3 * d, col0=g * 3 * d, tm=1024, tn=1024, dil=dil,
                                  name=f"qkv_b{i}_{g}")
                o_g, lse_g = dilated_group_attention(qkv, dil_bias, group=g, name=f"dil_attn{i}_{g}")
                outs.append(o_g)
                lses.append(lse_g)
            x = merge_project_residual(outs, lses, w_o_b, li, x, tm=256, name=f"wo_b{i}")

        kv = norm_matmul(mem, g_mem[i], w_kv_x, i,
                         _tile_heads(k_norm_x[i], X_HEADS).reshape(1, X_HEADS * HEAD_DIM),
                         n=2 * X_HEADS * HEAD_DIM, tm=N_SEQS * N_MEM, tn=X_HEADS * HEAD_DIM,
                         name=f"kv_x{i}")
        x = cross_attn_residual(x, g_cross[i], w_q_x, q_norm_x[i] * ATTN_SCALE,
                                kv.reshape(N_SEQS * N_MEM, 2 * X_HEADS * HEAD_DIM),
                                w_o_x, i, tm=512, name=f"cross{i}")
        x = mlp_residual(x, g_mlp[i], w_up, w_down, i,
                         tm=512, tf=1024, split_groups=(i == DEPTH - 1), name=f"mlp{i}")

    y_prompt, y_sample = x
    return (y_prompt.reshape(1, SEQ, d), y_sample.reshape(DEC_BATCH, DEC_SEQ, d))
```

```python
import functools
import math

import numpy as np
import jax
import jax.numpy as jnp
from jax import lax
from jax.experimental import pallas as pl
from jax.experimental.pallas import tpu as pltpu

D_MODEL = 2048
SEQ = 8192
DEPTH = 4
DEC_BATCH = 4
DEC_SEQ = 2048
N_TOK = SEQ + DEC_BATCH * DEC_SEQ
N_SEQS = 1 + DEC_BATCH

HEAD_DIM = 128
N_HEADS = D_MODEL // HEAD_DIM
NA_WIN_ROWS = 8
NA_WIN_COLS = 16
GRID_W = 64
DIL_CONFIGS = ((128, 1), (512, 4), (2048, 16))
N_GROUPS = len(DIL_CONFIGS)
DIL_RADIUS = 64
T5_BUCKETS = 32
T5_MAX_DIST = 1024
X_HEADS = 4
N_MEM = 256
D_FF = 4 * D_MODEL
RMS_EPS = 1e-6
ATTN_SCALE = 1.0 / math.sqrt(HEAD_DIM)
NEG_INF = float("-inf")

F32 = jnp.float32
BF16 = jnp.bfloat16
NT_DIMS = (((1,), (1,)), ((), ()))

VMEM_LIMIT_BYTES = 56 * 1024 * 1024

NA_QROWS = 4
NA_BLK = NA_QROWS * GRID_W
NA_KBLKS = 3
NA_NBP = SEQ // NA_BLK
NA_NBS = DEC_SEQ // NA_BLK

DIL_BQ = 128
DIL_KW = DIL_BQ + 2 * DIL_RADIUS

MATMUL_CHUNK = 256
DEINTERLEAVE_STRIDE = 4
SLAB_RING = 4


def _params(*sem):
    return pltpu.CompilerParams(dimension_semantics=sem, vmem_limit_bytes=VMEM_LIMIT_BYTES)


def _layer_spec(block, layer, index_map):
    return pl.BlockSpec((None,) + block, lambda *ids: (layer,) + index_map(*ids))


def _group_row_specs(tm, width):
    tp = SEQ // tm
    return tp, [pl.BlockSpec((tm, width), lambda i, *_: (jnp.minimum(i, tp - 1), 0)),
                pl.BlockSpec((tm, width), lambda i, *_: (jnp.maximum(i - tp, 0), 0))]


def _norm_rows(x, g):
    ms = jnp.mean(x * x, axis=-1, keepdims=True)
    return x * lax.rsqrt(ms + RMS_EPS) * g


def _head_slice(h):
    return slice(h * HEAD_DIM, (h + 1) * HEAD_DIM)


def _softmax_pv(logits, values, ones=None):
    m = functools.reduce(jnp.maximum, [jnp.max(s, axis=-1, keepdims=True) for s in logits])
    if ones is None:
        probs = [jnp.exp(s - m) for s in logits]
        den = functools.reduce(jnp.add, [jnp.sum(p, axis=-1, keepdims=True) for p in probs])
        acc = functools.reduce(jnp.add, [jnp.dot(p.astype(BF16), v, preferred_element_type=F32)
                                         for p, v in zip(probs, values)])
        return acc / den, m, den
    res = functools.reduce(jnp.add, [
        jnp.dot(jnp.exp((s - m).astype(BF16)), jnp.concatenate([v, ones], axis=1),
                preferred_element_type=F32)
        for s, v in zip(logits, values)])
    den = res[:, HEAD_DIM:]
    return res[:, :HEAD_DIM] / den, m, den[:, :1]


def _norm_matmul_body(*refs, n_norm_blocks, n_blocks, tm, tn, dil, prompt_tiles):
    n_x = 1 if prompt_tiles is None else 2
    x_refs = refs[:n_x]
    g_ref, w_ref, hg_ref, o_ref, xn_ref = refs[n_x:n_x + 5]
    slab_refs = refs[n_x + 5:]
    i = pl.program_id(0)
    j = pl.program_id(1)

    def normalise(x_ref):
        xn_ref[...] = _norm_rows(x_ref[...], g_ref[...]).astype(BF16)

    if prompt_tiles is None:
        pl.when(j == 0)(functools.partial(normalise, x_refs[0]))
    else:
        pl.when((j == 0) & (i < prompt_tiles))(functools.partial(normalise, x_refs[0]))
        pl.when((j == 0) & (i >= prompt_tiles))(functools.partial(normalise, x_refs[1]))

    def run(head_norm):
        for c in range(tn // MATMUL_CHUNK):
            acc = jnp.dot(xn_ref[...], w_ref[:, c * MATMUL_CHUNK:(c + 1) * MATMUL_CHUNK],
                          preferred_element_type=F32)
            for s in range(MATMUL_CHUNK // HEAD_DIM):
                slab = c * (MATMUL_CHUNK // HEAD_DIM) + s
                sl = _head_slice(slab)
                piece = acc[:, _head_slice(s)]
                if head_norm:
                    piece = _norm_rows(piece, hg_ref[:, sl])
                if dil == 1:
                    o_ref[0, :, sl] = piece.astype(o_ref.dtype)
                    continue
                slab_ref = slab_refs[0]
                slot = slab % SLAB_RING
                slab_ref[slot] = piece
                if dil == DEINTERLEAVE_STRIDE:
                    for r in range(dil):
                        rows = slab_ref[slot, pl.ds(r, tm // dil, stride=dil), :]
                        o_ref[r, :, sl] = rows.astype(o_ref.dtype)
                else:
                    quarter_ref = slab_refs[1]
                    outer = dil // DEINTERLEAVE_STRIDE
                    for b in range(DEINTERLEAVE_STRIDE):
                        quarter_ref[slot, b] = slab_ref[
                            slot, pl.ds(b, tm // DEINTERLEAVE_STRIDE, stride=DEINTERLEAVE_STRIDE), :]
                        for a in range(outer):
                            rows = quarter_ref[slot, b, pl.ds(a, tm // dil, stride=outer), :]
                            o_ref[a * DEINTERLEAVE_STRIDE + b, :, sl] = rows.astype(o_ref.dtype)

    if n_norm_blocks == n_blocks:
        run(True)
    else:
        pl.when(j < n_norm_blocks)(functools.partial(run, True))
        pl.when(j >= n_norm_blocks)(functools.partial(run, False))


def norm_matmul(x, gains, gain_idx, w_stack, layer, head_gains, head_gain_idx, *, n, tm, tn, name,
                col0=0, dil=1):
    pair = isinstance(x, tuple)
    m = sum(part.shape[0] for part in x) if pair else x.shape[0]
    k = w_stack.shape[1]
    n_norm = head_gains.shape[2]
    n_blocks = n // tn
    n_norm_blocks = n_norm // tn
    assert m % tm == 0 and n % tn == 0 and n_norm % tn == 0 and n_norm_blocks >= 1 and col0 % tn == 0
    assert tn % MATMUL_CHUNK == 0 and tm % (16 * dil) == 0
    if pair:
        prompt_tiles, x_specs = _group_row_specs(tm, k)
        x_args = list(x)
    else:
        prompt_tiles, x_specs, x_args = None, [pl.BlockSpec((tm, k), lambda i, j: (i, 0))], [x]
    body = functools.partial(_norm_matmul_body, n_norm_blocks=n_norm_blocks, n_blocks=n_blocks,
                             tm=tm, tn=tn, dil=dil, prompt_tiles=prompt_tiles)
    scratch = [pltpu.VMEM((tm, k), BF16)]
    if dil > 1:
        assert dil % DEINTERLEAVE_STRIDE == 0 and dil // DEINTERLEAVE_STRIDE <= DEINTERLEAVE_STRIDE
        scratch.append(pltpu.VMEM((SLAB_RING, tm, HEAD_DIM), F32))
    if dil > DEINTERLEAVE_STRIDE:
        scratch.append(pltpu.VMEM((SLAB_RING, DEINTERLEAVE_STRIDE, tm // DEINTERLEAVE_STRIDE,
                                   HEAD_DIM), F32))
    return pl.pallas_call(
        body,
        grid=(m // tm, n_blocks),
        in_specs=x_specs + [
            _layer_spec((1, k), gain_idx, lambda i, j: (0, 0)),
            _layer_spec((k, tn), layer, lambda i, j: (0, col0 // tn + j)),
            _layer_spec((1, tn), head_gain_idx, lambda i, j: (0, jnp.minimum(j, n_norm_blocks - 1))),
        ],
        out_specs=pl.BlockSpec((dil, tm // dil, tn), lambda i, j: (0, i, j)),
        out_shape=jax.ShapeDtypeStruct((dil, m // dil, n), BF16),
        scratch_shapes=scratch,
        compiler_params=_params("arbitrary" if pair else "parallel", "arbitrary"),
        name=name,
    )(*x_args, gains, w_stack, head_gains)


def _matmul_residual_body(a_ref, w_ref, *refs, prompt_tiles):
    o_ref = refs[-1]
    if prompt_tiles is None:
        res = refs[0][...]
    else:
        res = jnp.where(pl.program_id(0) < prompt_tiles, refs[0][...], refs[1][...])
    o_ref[...] = res + jnp.dot(a_ref[...], w_ref[...], preferred_element_type=F32)


def matmul_residual(a, w_stack, layer, res, *, tm, name):
    m, k = a.shape
    n = w_stack.shape[2]
    assert m % tm == 0
    if isinstance(res, tuple):
        prompt_tiles, res_specs = _group_row_specs(tm, n)
        res_args = list(res)
    else:
        prompt_tiles, res_specs, res_args = None, [pl.BlockSpec((tm, n), lambda i: (i, 0))], [res]
    return pl.pallas_call(
        functools.partial(_matmul_residual_body, prompt_tiles=prompt_tiles),
        grid=(m // tm,),
        in_specs=[pl.BlockSpec((tm, k), lambda i: (i, 0)),
                  _layer_spec((k, n), layer, lambda i: (0, 0))] + res_specs,
        out_specs=pl.BlockSpec((tm, n), lambda i: (i, 0)),
        out_shape=jax.ShapeDtypeStruct((m, n), F32),
        compiler_params=_params("arbitrary" if prompt_tiles is not None else "parallel"),
        name=name,
    )(a, w_stack, *res_args)


def _mlp_body(x_ref, g_ref, wu_ref, wd_ref, *refs, prompt_tiles):
    xn_ref = refs[-1]

    def step(o_ref):
        @pl.when(pl.program_id(1) == 0)
        def _():
            x = x_ref[...]
            xn_ref[...] = _norm_rows(x, g_ref[...]).astype(BF16)
            o_ref[...] = x

        h = jnp.dot(xn_ref[...], wu_ref[...], preferred_element_type=F32)
        h = jnp.maximum(h, 0.0)
        o_ref[...] += jnp.dot((h * h).astype(BF16), wd_ref[...], preferred_element_type=F32)

    if prompt_tiles is None:
        step(refs[0])
    else:
        i = pl.program_id(0)
        pl.when(i < prompt_tiles)(functools.partial(step, refs[0]))
        pl.when(i >= prompt_tiles)(functools.partial(step, refs[1]))


def mlp_residual(x, gains, w_up, w_down, layer, *, tm, tf, name, split_groups=False):
    m, d = x.shape
    ff = w_up.shape[2]
    assert m % tm == 0 and ff % tf == 0 and SEQ % tm == 0
    if split_groups:
        tp = SEQ // tm
        out_specs = [pl.BlockSpec((tm, d), lambda i, f: (jnp.minimum(i, tp - 1), 0)),
                     pl.BlockSpec((tm, d), lambda i, f: (jnp.maximum(i - tp, 0), 0))]
        out_shape = [jax.ShapeDtypeStruct((SEQ, d), F32), jax.ShapeDtypeStruct((m - SEQ, d), F32)]
    else:
        tp = None
        out_specs = pl.BlockSpec((tm, d), lambda i, f: (i, 0))
        out_shape = jax.ShapeDtypeStruct((m, d), F32)
    return pl.pallas_call(
        functools.partial(_mlp_body, prompt_tiles=tp),
        grid=(m // tm, ff // tf),
        in_specs=[
            pl.BlockSpec((tm, d), lambda i, f: (i, 0)),
            _layer_spec((1, d), layer, lambda i, f: (0, 0)),
            _layer_spec((d, tf), layer, lambda i, f: (0, f)),
            _layer_spec((tf, d), layer, lambda i, f: (f, 0)),
        ],
        out_specs=out_specs,
        out_shape=out_shape,
        scratch_shapes=[pltpu.VMEM((tm, d), BF16)],
        compiler_params=_params("arbitrary" if split_groups else "parallel", "arbitrary"),
        name=name,
    )(x, gains, w_up, w_down)


def _cross_body(x_ref, g_ref, wq_ref, qn_ref, k_ref, v_ref, wo_ref, o_ref):
    x = x_ref[...]
    xn = _norm_rows(x, g_ref[...]).astype(BF16)
    q = jnp.dot(xn, wq_ref[...], preferred_element_type=F32)
    heads = []
    for h in range(X_HEADS):
        sl = _head_slice(h)
        qh = _norm_rows(q[:, sl], qn_ref[...]).astype(BF16)
        s = lax.dot_general(qh, k_ref[:, sl], NT_DIMS, preferred_element_type=F32)
        oh, _, _ = _softmax_pv([s], [v_ref[:, sl]])
        heads.append(oh.astype(BF16))
    o = jnp.concatenate(heads, axis=-1)
    o_ref[...] = x + jnp.dot(o, wo_ref[...], preferred_element_type=F32)


def cross_attn_residual(x, gains, w_q, q_gains, kv, w_o, layer, *, tm, name):
    m, d = x.shape
    dx = X_HEADS * HEAD_DIM
    assert SEQ % tm == 0 and DEC_SEQ % tm == 0
    tiles_p = SEQ // tm
    tiles_s = DEC_SEQ // tm

    def mem_seq(i):
        return jnp.where(i < tiles_p, 0, 1 + (i - tiles_p) // tiles_s)

    return pl.pallas_call(
        _cross_body,
        grid=(m // tm,),
        in_specs=[
            pl.BlockSpec((tm, d), lambda i: (i, 0)),
            _layer_spec((1, d), layer, lambda i: (0, 0)),
            _layer_spec((d, dx), layer, lambda i: (0, 0)),
            _layer_spec((1, HEAD_DIM), layer, lambda i: (0, 0)),
            pl.BlockSpec((N_MEM, dx), lambda i: (mem_seq(i), 0)),
            pl.BlockSpec((N_MEM, dx), lambda i: (mem_seq(i), 1)),
            _layer_spec((dx, d), layer, lambda i: (0, 0)),
        ],
        out_specs=pl.BlockSpec((tm, d), lambda i: (i, 0)),
        out_shape=jax.ShapeDtypeStruct((m, d), F32),
        compiler_params=_params("parallel"),
        name=name,
    )(x, gains, w_q, q_gains, kv, kv, w_o)


def _na_block_info(i):
    in_prompt = i < NA_NBP
    local = jnp.where(in_prompt, i, (i - NA_NBP) % NA_NBS)
    nblk = jnp.where(in_prompt, NA_NBP, NA_NBS)
    base = jnp.where(in_prompt, 0, NA_NBP + ((i - NA_NBP) // NA_NBS) * NA_NBS)
    kblk = base + jnp.clip(local - 1, 0, nblk - NA_KBLKS)
    placement = jnp.where(local == 0, 0, jnp.where(local == nblk - 1, 2, 1))
    return kblk, placement


def _na_body(q_ref, k0_ref, k1_ref, k2_ref, v0_ref, v1_ref, v2_ref, b_ref, o_ref):
    k_refs = (k0_ref, k1_ref, k2_ref)
    v_refs = (v0_ref, v1_ref, v2_ref)
    ones = jnp.ones((NA_BLK, HEAD_DIM), BF16)
    for h in range(N_HEADS):
        sl = _head_slice(h)
        q = q_ref[:, sl]
        logits = []
        for c in range(NA_KBLKS):
            s = lax.dot_general(q, k_refs[c][:, sl], NT_DIMS, preferred_element_type=F32)
            logits.append(s + b_ref[0, h, :, c * NA_BLK:(c + 1) * NA_BLK])
        o, _, _ = _softmax_pv(logits, [v_ref[:, sl] for v_ref in v_refs], ones)
        o_ref[:, sl] = o.astype(o_ref.dtype)


def neighborhood_attention(qkv, bias_tables, layer, *, name):
    d = D_MODEL

    def kv_spec(c, part):
        return pl.BlockSpec((NA_BLK, d), lambda i: (_na_block_info(i)[0] + c, part))

    return pl.pallas_call(
        _na_body,
        grid=(N_TOK // NA_BLK,),
        in_specs=[pl.BlockSpec((NA_BLK, d), lambda i: (i, 0))]
        + [kv_spec(c, 1) for c in range(NA_KBLKS)]
        + [kv_spec(c, 2) for c in range(NA_KBLKS)]
        + [pl.BlockSpec((None, 1, N_HEADS, NA_BLK, NA_KBLKS * NA_BLK),
                        lambda i: (layer, _na_block_info(i)[1], 0, 0, 0))],
        out_specs=pl.BlockSpec((NA_BLK, d), lambda i: (i, 0)),
        out_shape=jax.ShapeDtypeStruct((N_TOK, d), BF16),
        compiler_params=_params("parallel"),
        name=name,
    )(*([qkv] * (1 + 2 * NA_KBLKS)), bias_tables)


def _toeplitz(v, n_rows, n_cols):
    period = v.shape[-1]
    assert period >= n_rows + n_cols - 1
    flat = jnp.tile(v, (1,) * (v.ndim - 1) + (n_rows,))[..., :n_rows * (period - 1)]
    return flat.reshape(v.shape[:-1] + (n_rows, period - 1))[..., :n_cols]


def na_bias_tables(rpb):
    n_layers = rpb.shape[0]
    n_krows = NA_KBLKS * NA_QROWS
    n_dr = 2 * NA_WIN_ROWS - 1
    c = np.arange(GRID_W)
    cs = np.clip(c - NA_WIN_COLS // 2, 0, GRID_W - NA_WIN_COLS)
    col_ok = (c[None, :] >= cs[:, None]) & (c[None, :] < cs[:, None] + NA_WIN_COLS)
    dist = np.arange(2 * GRID_W)
    dist = np.where(dist < GRID_W, dist, dist - 2 * GRID_W)
    by_dist = rpb.astype(F32)[..., np.clip(dist, -(NA_WIN_COLS - 1), NA_WIN_COLS - 1) + (NA_WIN_COLS - 1)]
    tiles = jnp.where(col_ok, _toeplitz(by_dist, GRID_W, GRID_W), NEG_INF)
    tiles = jnp.concatenate([tiles, jnp.full_like(tiles[:, :, :1], NEG_INF)], axis=2)

    def row_offset(placement, rq, rk):
        if placement == 0:
            return rk - rq + 7 if rk <= 7 else n_dr
        if placement == 1:
            return rk - rq + 3 if rq <= rk <= rq + 7 else n_dr
        return rk - rq - 1 if rk >= 4 else n_dr

    dr = np.array([[[row_offset(p, rq, rk) for rk in range(n_krows)]
                    for rq in range(NA_QROWS)] for p in range(3)])
    t = tiles[:, :, dr]
    t = t.transpose(0, 2, 1, 3, 5, 4, 6)
    return t.reshape(n_layers, 3, N_HEADS, NA_BLK, NA_KBLKS * NA_BLK)


def _dil_body(q_ref, k_ref, v_ref, kp_ref, kn_ref, vp_ref, vn_ref, b_ref, o_ref, lse_ref,
              kext_ref, vext_ref, *, dil, tl):
    lse_ref[...] = jnp.zeros_like(lse_ref)

    row0 = pl.program_id(1) * tl
    rows_prompt = SEQ // dil
    len_sample = DEC_SEQ // dil
    in_prompt = row0 < rows_prompt
    pos0 = jnp.where(in_prompt, row0, (row0 - rows_prompt) % len_sample)
    seq_len = jnp.where(in_prompt, rows_prompt, len_sample)

    for ext_ref, prev_ref, cur_ref, next_ref in ((kext_ref, kp_ref, k_ref, kn_ref),
                                                 (vext_ref, vp_ref, v_ref, vn_ref)):
        ext_ref[0:DIL_RADIUS] = prev_ref[...]
        ext_ref[DIL_RADIUS:DIL_RADIUS + tl] = cur_ref[...]
        ext_ref[DIL_RADIUS + tl:] = next_ref[...]

    for qs in range(0, tl, DIL_BQ):
        rows = slice(qs, qs + DIL_BQ)
        win = slice(qs, qs + DIL_KW)
        kpos = pos0 + qs - DIL_RADIUS + lax.broadcasted_iota(jnp.int32, (1, DIL_KW), 1)
        edge = jnp.where((kpos >= 0) & (kpos < seq_len), 0.0, NEG_INF).astype(F32)
        for h in range(N_HEADS):
            sl = _head_slice(h)
            s = lax.dot_general(q_ref[rows, sl], kext_ref[win, sl], NT_DIMS,
                                preferred_element_type=F32)
            s = s + b_ref[0, h] + edge
            o, m, den = _softmax_pv([s], [vext_ref[win, sl]])
            o_ref[rows, sl] = o
            lse_ref[rows, h:h + 1] = m + jnp.log(den)


def dilated_group_attention(qkv, bias_table, *, group, name):
    d = D_MODEL
    dil = DIL_CONFIGS[group][1]
    assert DIL_CONFIGS[group][0] // (2 * dil) == DIL_RADIUS
    rows = N_TOK // dil
    tl = min(512, DEC_SEQ // dil)
    assert (SEQ // dil) % tl == 0 and (DEC_SEQ // dil) % tl == 0 and tl % DIL_BQ == 0
    halo_per_tile = tl // DIL_RADIUS
    n_halo = rows // DIL_RADIUS

    def main_spec(part):
        return pl.BlockSpec((None, tl, d), lambda r, t: (r, t, part))

    def prev_spec(part):
        return pl.BlockSpec((None, DIL_RADIUS, d),
                            lambda r, t: (r, jnp.maximum(t * halo_per_tile - 1, 0), part))

    def next_spec(part):
        return pl.BlockSpec((None, DIL_RADIUS, d),
                            lambda r, t: (r, jnp.minimum((t + 1) * halo_per_tile, n_halo - 1), part))

    return pl.pallas_call(
        functools.partial(_dil_body, dil=dil, tl=tl),
        grid=(dil, rows // tl),
        in_specs=[main_spec(0), main_spec(1), main_spec(2),
                  prev_spec(1), next_spec(1), prev_spec(2), next_spec(2),
                  pl.BlockSpec((1, N_HEADS, DIL_BQ, DIL_KW), lambda r, t: (group, 0, 0, 0))],
        out_specs=[pl.BlockSpec((None, tl, d), lambda r, t: (r, t, 0)),
                   pl.BlockSpec((None, tl, HEAD_DIM), lambda r, t: (r, t, 0))],
        out_shape=[jax.ShapeDtypeStruct((dil, rows, d), F32),
                   jax.ShapeDtypeStruct((dil, rows, HEAD_DIM), F32)],
        scratch_shapes=[pltpu.VMEM((tl + 2 * DIL_RADIUS, d), BF16),
                        pltpu.VMEM((tl + 2 * DIL_RADIUS, d), BF16)],
        compiler_params=_params("parallel", "parallel"),
        name=name,
    )(*([qkv] * 7), bias_table)


def _t5_bucket(rel):
    nb = T5_BUCKETS // 2
    max_exact = nb // 2
    ret = jnp.where(rel > 0, nb, 0)
    n = jnp.abs(rel)
    n_f = jnp.maximum(n, 1).astype(F32)
    large = max_exact + (jnp.log(n_f / max_exact) / math.log(T5_MAX_DIST / max_exact)
                         * (nb - max_exact)).astype(jnp.int32)
    large = jnp.minimum(large, nb - 1)
    return ret + jnp.where(n < max_exact, n, large)


def dilated_bias_table(t5_table):
    period = DIL_BQ + DIL_KW
    rel = np.arange(-DIL_RADIUS, DIL_RADIUS + 1)
    pad = jnp.full((N_HEADS, period - rel.size), NEG_INF, F32)
    tables = []
    for g, (_, dil) in enumerate(DIL_CONFIGS):
        band = t5_table[:, g].astype(F32)[_t5_bucket(jnp.asarray(rel * dil, jnp.int32))]
        tables.append(_toeplitz(jnp.concatenate([band.T, pad], axis=-1), DIL_BQ, DIL_KW))
    return jnp.stack(tables)


def _merge_project_body(*refs, tm):
    o_refs = refs[:N_GROUPS]
    l_refs = refs[N_GROUPS:2 * N_GROUPS]
    w_ref, r_ref, out_ref, ltok_ref, otok_ref, a_ref = refs[2 * N_GROUPS:]
    dils = [dil for _, dil in DIL_CONFIGS]
    assert dils[0] == 1

    def to_token_order(dst_ref, idx, src, dil, r):
        if dil == 1:
            dst_ref[idx] = src
        else:
            dst_ref[idx + (pl.ds(r, tm // dil, stride=dil), slice(None))] = src

    for g, dil in enumerate(dils):
        for r in range(dil):
            to_token_order(ltok_ref, (g,), l_refs[g][r], dil, r)
    lses = [ltok_ref[g] for g in range(N_GROUPS)]
    top = functools.reduce(jnp.maximum, lses)
    wts = [jnp.exp(l - top) for l in lses]
    z = functools.reduce(jnp.add, wts)
    wts = [w / z for w in wts]

    for g, dil in enumerate(dils):
        if dil == 1:
            continue
        for r in range(dil):
            for h in range(N_HEADS):
                to_token_order(otok_ref, (g - 1, h), o_refs[g][r, :, _head_slice(h)], dil, r)
    for h in range(N_HEADS):
        sl = _head_slice(h)
        parts = [o_refs[0][0, :, sl]] + [otok_ref[g - 1, h] for g in range(1, N_GROUPS)]
        merged = functools.reduce(jnp.add, [p * wts[g][:, h:h + 1] for g, p in enumerate(parts)])
        a_ref[:, sl] = merged.astype(BF16)
    out_ref[...] = r_ref[...] + jnp.dot(a_ref[...], w_ref[...], preferred_element_type=F32)


def merge_project_residual(outs, lses, w_stack, layer, res, *, tm, name):
    m, d = res.shape
    dils = [dil for _, dil in DIL_CONFIGS]
    assert m % tm == 0 and all(tm % (8 * dil) == 0 for dil in dils)
    in_specs = [pl.BlockSpec((dil, tm // dil, d), lambda i: (0, i, 0)) for dil in dils]
    in_specs += [pl.BlockSpec((dil, tm // dil, HEAD_DIM), lambda i: (0, i, 0)) for dil in dils]
    in_specs += [_layer_spec((d, d), layer, lambda i: (0, 0)), pl.BlockSpec((tm, d), lambda i: (i, 0))]
    return pl.pallas_call(
        functools.partial(_merge_project_body, tm=tm),
        grid=(m // tm,),
        in_specs=in_specs,
        out_specs=pl.BlockSpec((tm, d), lambda i: (i, 0)),
        out_shape=jax.ShapeDtypeStruct((m, d), F32),
        scratch_shapes=[pltpu.VMEM((N_GROUPS, tm, HEAD_DIM), F32),
                        pltpu.VMEM((N_GROUPS - 1, N_HEADS, tm, HEAD_DIM), F32),
                        pltpu.VMEM((tm, d), BF16)],
        compiler_params=_params("parallel"),
        name=name,
    )(*outs, *lses, w_stack, res)


def _qk_gains(q_gains, k_gains):
    tiled = [jnp.tile(g.astype(F32).reshape(-1, HEAD_DIM), (1, N_HEADS)) for g in (q_gains, k_gains)]
    return jnp.concatenate([tiled[0] * ATTN_SCALE, tiled[1]], axis=1)[:, None]


def kernel(x_prompt, x_sample, mem_prompt, mem_sample, g_mix, g_cross, g_mem, g_mlp, w_qkv_a, q_norm_a, k_norm_a, rpb_a, w_o_a, w_qkv_b, q_norm_b, k_norm_b, t5_table, w_o_b, w_q_x, w_kv_x, q_norm_x, k_norm_x, w_o_x, w_up, w_down):
    d = D_MODEL
    x = (x_prompt.reshape(SEQ, d), x_sample.reshape(DEC_BATCH * DEC_SEQ, d))
    mem = jnp.concatenate([mem_prompt.reshape(N_MEM, d), mem_sample.reshape(DEC_BATCH * N_MEM, d)])
    dil_bias = dilated_bias_table(t5_table)
    na_bias = na_bias_tables(rpb_a)
    (w_qkv_a, w_o_a, w_qkv_b, w_o_b, w_q_x, w_kv_x, w_o_x, w_up, w_down) = (
        w.astype(BF16) for w in (w_qkv_a, w_o_a, w_qkv_b, w_o_b, w_q_x, w_kv_x, w_o_x, w_up, w_down))
    g_mix, g_cross, g_mem, g_mlp = (g.astype(F32)[:, None] for g in (g_mix, g_cross, g_mem, g_mlp))
    qk_gain_a = _qk_gains(q_norm_a, k_norm_a)
    qk_gain_b = _qk_gains(q_norm_b, k_norm_b)
    k_gain_x = jnp.tile(k_norm_x.astype(F32), (1, X_HEADS))[:, None]
    q_gain_x = (q_norm_x.astype(F32) * ATTN_SCALE)[:, None]

    for i in range(DEPTH):
        li = i // 2
        if i % 2 == 0:
            qkv = norm_matmul(x, g_mix, i, w_qkv_a, li, qk_gain_a, li,
                              n=3 * d, tm=1024, tn=1024 if isinstance(x, tuple) else 2048,
                              name=f"qkv_a{i}")
            o = neighborhood_attention(qkv.reshape(N_TOK, 3 * d), na_bias, li, name=f"na_attn{i}")
            x = matmul_residual(o, w_o_a, li, x, tm=512, name=f"wo_a{i}")
        else:
            outs, lses = [], []
            for g, (_, dil) in enumerate(DIL_CONFIGS):
                qkv = norm_matmul(x, g_mix, i, w_qkv_b, li, qk_gain_b, li * N_GROUPS + g,
                                  n=3 * d, col0=g * 3 * d, tm=1024, tn=2048, dil=dil,
                                  name=f"qkv_b{i}_{g}")
                o_g, lse_g = dilated_group_attention(qkv, dil_bias, group=g, name=f"dil_attn{i}_{g}")
                outs.append(o_g)
                lses.append(lse_g)
            x = merge_project_residual(outs, lses, w_o_b, li, x, tm=256, name=f"wo_b{i}")

        kv = norm_matmul(mem, g_mem, i, w_kv_x, i, k_gain_x, i,
                         n=2 * X_HEADS * HEAD_DIM, tm=N_SEQS * N_MEM, tn=X_HEADS * HEAD_DIM,
                         name=f"kv_x{i}")
        x = cross_attn_residual(x, g_cross, w_q_x, q_gain_x,
                                kv.reshape(N_SEQS * N_MEM, 2 * X_HEADS * HEAD_DIM),
                                w_o_x, i, tm=512, name=f"cross{i}")
        x = mlp_residual(x, g_mlp, w_up, w_down, i,
                         tm=512, tf=1024, split_groups=(i == DEPTH - 1), name=f"mlp{i}")

    y_prompt, y_sample = x
    return (y_prompt.reshape(1, SEQ, d), y_sample.reshape(DEC_BATCH, DEC_SEQ, d))
```

```python
import functools
import math

import numpy as np
import jax
import jax.numpy as jnp
from jax import lax
from jax.experimental import pallas as pl
from jax.experimental.pallas import tpu as pltpu

D_MODEL = 2048
SEQ = 8192
DEPTH = 4
DEC_BATCH = 4
DEC_SEQ = 2048
N_TOK = SEQ + DEC_BATCH * DEC_SEQ
N_SEQS = 1 + DEC_BATCH

HEAD_DIM = 128
N_HEADS = D_MODEL // HEAD_DIM
NA_WIN_ROWS = 8
NA_WIN_COLS = 16
GRID_W = 64
DIL_CONFIGS = ((128, 1), (512, 4), (2048, 16))
N_GROUPS = len(DIL_CONFIGS)
DIL_RADIUS = 64
T5_BUCKETS = 32
T5_MAX_DIST = 1024
X_HEADS = 4
N_MEM = 256
D_FF = 4 * D_MODEL
RMS_EPS = 1e-6
ATTN_SCALE = 1.0 / math.sqrt(HEAD_DIM)
NEG_INF = float("-inf")

F32 = jnp.float32
BF16 = jnp.bfloat16
NT_DIMS = (((1,), (1,)), ((), ()))

VMEM_LIMIT_BYTES = 56 * 1024 * 1024

NA_QROWS = 4
NA_BLK = NA_QROWS * GRID_W
NA_KBLKS = 3
NA_NBP = SEQ // NA_BLK
NA_NBS = DEC_SEQ // NA_BLK

DIL_BQ = 128
DIL_KW = DIL_BQ + 2 * DIL_RADIUS
DIL_STEP_ROWS = 512

MATMUL_CHUNK = 256
DEINTERLEAVE_STRIDE = 4
SLAB_RING = 4


def _params(*sem):
    return pltpu.CompilerParams(dimension_semantics=sem, vmem_limit_bytes=VMEM_LIMIT_BYTES)


def _layer_spec(block, layer, index_map):
    return pl.BlockSpec((None,) + block, lambda *ids: (layer,) + index_map(*ids))


def _group_row_specs(tm, width):
    tp = SEQ // tm
    return tp, [pl.BlockSpec((tm, width), lambda i, *_: (jnp.minimum(i, tp - 1), 0)),
                pl.BlockSpec((tm, width), lambda i, *_: (jnp.maximum(i - tp, 0), 0))]


def _norm_rows(x, g):
    ms = jnp.mean(x * x, axis=-1, keepdims=True)
    return x * lax.rsqrt(ms + RMS_EPS) * g


def _head_slice(h):
    return slice(h * HEAD_DIM, (h + 1) * HEAD_DIM)


def _softmax_pv(logits, values, ones=None):
    m = functools.reduce(jnp.maximum, [jnp.max(s, axis=-1, keepdims=True) for s in logits])
    if ones is None:
        probs = [jnp.exp(s - m) for s in logits]
        den = functools.reduce(jnp.add, [jnp.sum(p, axis=-1, keepdims=True) for p in probs])
        acc = functools.reduce(jnp.add, [jnp.dot(p.astype(BF16), v, preferred_element_type=F32)
                                         for p, v in zip(probs, values)])
        return acc / den, m, den
    res = functools.reduce(jnp.add, [
        jnp.dot(jnp.exp((s - m).astype(BF16)), jnp.concatenate([v, ones], axis=1),
                preferred_element_type=F32)
        for s, v in zip(logits, values)])
    den = res[:, HEAD_DIM:]
    return res[:, :HEAD_DIM] / den, m, den[:, :1]


def _norm_matmul_body(*refs, n_norm_blocks, n_blocks, tm, tn, dil, prompt_tiles):
    n_x = 1 if prompt_tiles is None else 2
    x_refs = refs[:n_x]
    g_ref, w_ref, hg_ref, o_ref, xn_ref = refs[n_x:n_x + 5]
    slab_refs = refs[n_x + 5:]
    i = pl.program_id(0)
    j = pl.program_id(1)

    def normalise(x_ref):
        xn_ref[...] = _norm_rows(x_ref[...], g_ref[...]).astype(BF16)

    if prompt_tiles is None:
        pl.when(j == 0)(functools.partial(normalise, x_refs[0]))
    else:
        pl.when((j == 0) & (i < prompt_tiles))(functools.partial(normalise, x_refs[0]))
        pl.when((j == 0) & (i >= prompt_tiles))(functools.partial(normalise, x_refs[1]))

    def run(head_norm):
        for c in range(tn // MATMUL_CHUNK):
            acc = jnp.dot(xn_ref[...], w_ref[:, c * MATMUL_CHUNK:(c + 1) * MATMUL_CHUNK],
                          preferred_element_type=F32)
            for s in range(MATMUL_CHUNK // HEAD_DIM):
                slab = c * (MATMUL_CHUNK // HEAD_DIM) + s
                sl = _head_slice(slab)
                piece = acc[:, _head_slice(s)]
                if head_norm:
                    piece = _norm_rows(piece, hg_ref[:, sl])
                if dil == 1:
                    o_ref[0, :, sl] = piece.astype(o_ref.dtype)
                    continue
                slab_ref = slab_refs[0]
                slot = slab % SLAB_RING
                slab_ref[slot] = piece
                if dil == DEINTERLEAVE_STRIDE:
                    for r in range(dil):
                        rows = slab_ref[slot, pl.ds(r, tm // dil, stride=dil), :]
                        o_ref[r, :, sl] = rows.astype(o_ref.dtype)
                else:
                    quarter_ref = slab_refs[1]
                    outer = dil // DEINTERLEAVE_STRIDE
                    for b in range(DEINTERLEAVE_STRIDE):
                        quarter_ref[slot, b] = slab_ref[
                            slot, pl.ds(b, tm // DEINTERLEAVE_STRIDE, stride=DEINTERLEAVE_STRIDE), :]
                        for a in range(outer):
                            rows = quarter_ref[slot, b, pl.ds(a, tm // dil, stride=outer), :]
                            o_ref[a * DEINTERLEAVE_STRIDE + b, :, sl] = rows.astype(o_ref.dtype)

    if n_norm_blocks == n_blocks:
        run(True)
    else:
        pl.when(j < n_norm_blocks)(functools.partial(run, True))
        pl.when(j >= n_norm_blocks)(functools.partial(run, False))


def norm_matmul(x, gains, gain_idx, w_stack, layer, head_gains, head_gain_idx, *, n, tm, tn, name,
                col0=0, dil=1):
    pair = isinstance(x, tuple)
    m = sum(part.shape[0] for part in x) if pair else x.shape[0]
    k = w_stack.shape[1]
    n_norm = head_gains.shape[2]
    n_blocks = n // tn
    n_norm_blocks = n_norm // tn
    assert m % tm == 0 and n % tn == 0 and n_norm % tn == 0 and n_norm_blocks >= 1 and col0 % tn == 0
    assert tn % MATMUL_CHUNK == 0 and tm % (16 * dil) == 0
    if pair:
        prompt_tiles, x_specs = _group_row_specs(tm, k)
        x_args = list(x)
    else:
        prompt_tiles, x_specs, x_args = None, [pl.BlockSpec((tm, k), lambda i, j: (i, 0))], [x]
    body = functools.partial(_norm_matmul_body, n_norm_blocks=n_norm_blocks, n_blocks=n_blocks,
                             tm=tm, tn=tn, dil=dil, prompt_tiles=prompt_tiles)
    scratch = [pltpu.VMEM((tm, k), BF16)]
    if dil > 1:
        assert dil % DEINTERLEAVE_STRIDE == 0 and dil // DEINTERLEAVE_STRIDE <= DEINTERLEAVE_STRIDE
        scratch.append(pltpu.VMEM((SLAB_RING, tm, HEAD_DIM), F32))
    if dil > DEINTERLEAVE_STRIDE:
        scratch.append(pltpu.VMEM((SLAB_RING, DEINTERLEAVE_STRIDE, tm // DEINTERLEAVE_STRIDE,
                                   HEAD_DIM), F32))
    return pl.pallas_call(
        body,
        grid=(m // tm, n_blocks),
        in_specs=x_specs + [
            _layer_spec((1, k), gain_idx, lambda i, j: (0, 0)),
            _layer_spec((k, tn), layer, lambda i, j: (0, col0 // tn + j)),
            _layer_spec((1, tn), head_gain_idx, lambda i, j: (0, jnp.minimum(j, n_norm_blocks - 1))),
        ],
        out_specs=pl.BlockSpec((dil, tm // dil, tn), lambda i, j: (0, i, j)),
        out_shape=jax.ShapeDtypeStruct((dil, m // dil, n), BF16),
        scratch_shapes=scratch,
        compiler_params=_params("arbitrary" if pair else "parallel", "arbitrary"),
        name=name,
    )(*x_args, gains, w_stack, head_gains)


def _project_cross_body(a_ref, w_ref, *refs, prompt_tiles):
    o_ref = refs[-1]
    if prompt_tiles is None:
        res, cross_refs = refs[0][...], refs[1:-1]
    else:
        res = jnp.where(pl.program_id(0) < prompt_tiles, refs[0][...], refs[1][...])
        cross_refs = refs[2:-1]
    x = res + jnp.dot(a_ref[...], w_ref[...], preferred_element_type=F32)
    o_ref[...] = _cross_residual(x, *cross_refs)


def project_cross_residual(a, w_stack, layer, res, cross_args, cross_layer, *, tm, name):
    m, k = a.shape
    n = w_stack.shape[2]
    assert m % tm == 0
    if isinstance(res, tuple):
        prompt_tiles, res_specs = _group_row_specs(tm, n)
        res_args = list(res)
    else:
        prompt_tiles, res_specs, res_args = None, [pl.BlockSpec((tm, n), lambda i: (i, 0))], [res]
    return pl.pallas_call(
        functools.partial(_project_cross_body, prompt_tiles=prompt_tiles),
        grid=(m // tm,),
        in_specs=[pl.BlockSpec((tm, k), lambda i: (i, 0)),
                  _layer_spec((k, n), layer, lambda i: (0, 0))] + res_specs
        + _cross_specs(tm, cross_layer),
        out_specs=pl.BlockSpec((tm, n), lambda i: (i, 0)),
        out_shape=jax.ShapeDtypeStruct((m, n), F32),
        compiler_params=_params("arbitrary" if prompt_tiles is not None else "parallel"),
        name=name,
    )(a, w_stack, *res_args, *cross_args)


def _mlp_body(x_ref, g_ref, wu_ref, wd_ref, *refs, prompt_tiles):
    xn_ref = refs[-1]

    def step(o_ref):
        @pl.when(pl.program_id(1) == 0)
        def _():
            x = x_ref[...]
            xn_ref[...] = _norm_rows(x, g_ref[...]).astype(BF16)
            o_ref[...] = x

        h = jnp.dot(xn_ref[...], wu_ref[...], preferred_element_type=F32)
        h = jnp.maximum(h, 0.0)
        o_ref[...] += jnp.dot((h * h).astype(BF16), wd_ref[...], preferred_element_type=F32)

    if prompt_tiles is None:
        step(refs[0])
    else:
        i = pl.program_id(0)
        pl.when(i < prompt_tiles)(functools.partial(step, refs[0]))
        pl.when(i >= prompt_tiles)(functools.partial(step, refs[1]))


def mlp_residual(x, gains, w_up, w_down, layer, *, tm, tf, name, split_groups=False):
    m, d = x.shape
    ff = w_up.shape[2]
    assert m % tm == 0 and ff % tf == 0 and SEQ % tm == 0
    if split_groups:
        tp = SEQ // tm
        out_specs = [pl.BlockSpec((tm, d), lambda i, f: (jnp.minimum(i, tp - 1), 0)),
                     pl.BlockSpec((tm, d), lambda i, f: (jnp.maximum(i - tp, 0), 0))]
        out_shape = [jax.ShapeDtypeStruct((SEQ, d), F32), jax.ShapeDtypeStruct((m - SEQ, d), F32)]
    else:
        tp = None
        out_specs = pl.BlockSpec((tm, d), lambda i, f: (i, 0))
        out_shape = jax.ShapeDtypeStruct((m, d), F32)
    return pl.pallas_call(
        functools.partial(_mlp_body, prompt_tiles=tp),
        grid=(m // tm, ff // tf),
        in_specs=[
            pl.BlockSpec((tm, d), lambda i, f: (i, 0)),
            _layer_spec((1, d), layer, lambda i, f: (0, 0)),
            _layer_spec((d, tf), layer, lambda i, f: (0, f)),
            _layer_spec((tf, d), layer, lambda i, f: (f, 0)),
        ],
        out_specs=out_specs,
        out_shape=out_shape,
        scratch_shapes=[pltpu.VMEM((tm, d), BF16)],
        compiler_params=_params("arbitrary" if split_groups else "parallel", "arbitrary"),
        name=name,
    )(x, gains, w_up, w_down)


def _cross_residual(x, g_ref, wq_ref, qn_ref, k_ref, v_ref, wo_ref):
    xn = _norm_rows(x, g_ref[...]).astype(BF16)
    q = jnp.dot(xn, wq_ref[...], preferred_element_type=F32)
    heads = []
    for h in range(X_HEADS):
        sl = _head_slice(h)
        qh = _norm_rows(q[:, sl], qn_ref[...]).astype(BF16)
        s = lax.dot_general(qh, k_ref[:, sl], NT_DIMS, preferred_element_type=F32)
        oh, _, _ = _softmax_pv([s], [v_ref[:, sl]])
        heads.append(oh.astype(BF16))
    o = jnp.concatenate(heads, axis=-1)
    return x + jnp.dot(o, wo_ref[...], preferred_element_type=F32)


def _cross_specs(tm, layer):
    d = D_MODEL
    dx = X_HEADS * HEAD_DIM
    assert SEQ % tm == 0 and DEC_SEQ % tm == 0
    tiles_p = SEQ // tm
    tiles_s = DEC_SEQ // tm

    def mem_seq(i):
        return jnp.where(i < tiles_p, 0, 1 + (i - tiles_p) // tiles_s)

    return [
        _layer_spec((1, d), layer, lambda i: (0, 0)),
        _layer_spec((d, dx), layer, lambda i: (0, 0)),
        _layer_spec((1, HEAD_DIM), layer, lambda i: (0, 0)),
        pl.BlockSpec((N_MEM, dx), lambda i: (mem_seq(i), 0)),
        pl.BlockSpec((N_MEM, dx), lambda i: (mem_seq(i), 1)),
        _layer_spec((dx, d), layer, lambda i: (0, 0)),
    ]


def _na_block_info(i):
    in_prompt = i < NA_NBP
    local = jnp.where(in_prompt, i, (i - NA_NBP) % NA_NBS)
    nblk = jnp.where(in_prompt, NA_NBP, NA_NBS)
    base = jnp.where(in_prompt, 0, NA_NBP + ((i - NA_NBP) // NA_NBS) * NA_NBS)
    kblk = base + jnp.clip(local - 1, 0, nblk - NA_KBLKS)
    placement = jnp.where(local == 0, 0, jnp.where(local == nblk - 1, 2, 1))
    return kblk, placement


def _na_body(q_ref, k0_ref, k1_ref, k2_ref, v0_ref, v1_ref, v2_ref, b_ref, o_ref):
    k_refs = (k0_ref, k1_ref, k2_ref)
    v_refs = (v0_ref, v1_ref, v2_ref)
    ones = jnp.ones((NA_BLK, HEAD_DIM), BF16)
    for h in range(N_HEADS):
        sl = _head_slice(h)
        q = q_ref[:, sl]
        logits = []
        for c in range(NA_KBLKS):
            s = lax.dot_general(q, k_refs[c][:, sl], NT_DIMS, preferred_element_type=F32)
            logits.append(s + b_ref[0, h, :, c * NA_BLK:(c + 1) * NA_BLK])
        o, _, _ = _softmax_pv(logits, [v_ref[:, sl] for v_ref in v_refs], ones)
        o_ref[:, sl] = o.astype(o_ref.dtype)


def neighborhood_attention(qkv, bias_tables, layer, *, name):
    d = D_MODEL

    def kv_spec(c, part):
        return pl.BlockSpec((NA_BLK, d), lambda i: (_na_block_info(i)[0] + c, part))

    return pl.pallas_call(
        _na_body,
        grid=(N_TOK // NA_BLK,),
        in_specs=[pl.BlockSpec((NA_BLK, d), lambda i: (i, 0))]
        + [kv_spec(c, 1) for c in range(NA_KBLKS)]
        + [kv_spec(c, 2) for c in range(NA_KBLKS)]
        + [pl.BlockSpec((None, 1, N_HEADS, NA_BLK, NA_KBLKS * NA_BLK),
                        lambda i: (layer, _na_block_info(i)[1], 0, 0, 0))],
        out_specs=pl.BlockSpec((NA_BLK, d), lambda i: (i, 0)),
        out_shape=jax.ShapeDtypeStruct((N_TOK, d), BF16),
        compiler_params=_params("parallel"),
        name=name,
    )(*([qkv] * (1 + 2 * NA_KBLKS)), bias_tables)


def _toeplitz(v, n_rows, n_cols):
    period = v.shape[-1]
    assert period >= n_rows + n_cols - 1
    flat = jnp.tile(v, (1,) * (v.ndim - 1) + (n_rows,))[..., :n_rows * (period - 1)]
    return flat.reshape(v.shape[:-1] + (n_rows, period - 1))[..., :n_cols]


def na_bias_tables(rpb):
    n_layers = rpb.shape[0]
    n_krows = NA_KBLKS * NA_QROWS
    n_dr = 2 * NA_WIN_ROWS - 1
    assert n_krows - NA_WIN_ROWS == NA_QROWS
    c = np.arange(GRID_W)
    cs = np.clip(c - NA_WIN_COLS // 2, 0, GRID_W - NA_WIN_COLS)
    col_ok = (c[None, :] >= cs[:, None]) & (c[None, :] < cs[:, None] + NA_WIN_COLS)
    dist = np.arange(2 * GRID_W)
    dist = np.where(dist < GRID_W, dist, dist - 2 * GRID_W)
    by_dist = rpb.astype(F32)[..., np.clip(dist, -(NA_WIN_COLS - 1), NA_WIN_COLS - 1) + (NA_WIN_COLS - 1)]
    tiles = jnp.where(col_ok, _toeplitz(by_dist, GRID_W, GRID_W), NEG_INF)
    side_by_side = tiles.transpose(0, 1, 3, 2, 4).reshape(n_layers, N_HEADS, GRID_W, n_dr * GRID_W)

    def strip(first_dr, lead):
        window = side_by_side[..., first_dr * GRID_W:(first_dr + NA_WIN_ROWS) * GRID_W]
        pad = [(0, 0)] * 3 + [(lead * GRID_W, (n_krows - NA_WIN_ROWS - lead) * GRID_W)]
        return jnp.pad(window, pad, constant_values=NEG_INF)

    placements = (
        [strip(7 - rq, 0) for rq in range(NA_QROWS)],
        [strip(3, rq) for rq in range(NA_QROWS)],
        [strip(3 - rq, 4) for rq in range(NA_QROWS)],
    )
    return jnp.stack([jnp.concatenate(strips, axis=2) for strips in placements], axis=1)


def _dil_body(q_ref, k_ref, v_ref, kp_ref, kn_ref, vp_ref, vn_ref, b_ref, o_ref, lse_ref,
              kext_ref, vext_ref, *, dil, tl, planes):
    lse_ref[...] = jnp.zeros_like(lse_ref)

    row0 = pl.program_id(1) * tl
    rows_prompt = SEQ // dil
    len_sample = DEC_SEQ // dil
    in_prompt = row0 < rows_prompt
    pos0 = jnp.where(in_prompt, row0, (row0 - rows_prompt) % len_sample)
    seq_len = jnp.where(in_prompt, rows_prompt, len_sample)

    for p in range(planes):
        for ext_ref, prev_ref, cur_ref, next_ref in ((kext_ref, kp_ref, k_ref, kn_ref),
                                                     (vext_ref, vp_ref, v_ref, vn_ref)):
            ext_ref[p, 0:DIL_RADIUS] = prev_ref[p]
            ext_ref[p, DIL_RADIUS:DIL_RADIUS + tl] = cur_ref[p]
            ext_ref[p, DIL_RADIUS + tl:] = next_ref[p]

        for qs in range(0, tl, DIL_BQ):
            rows = slice(qs, qs + DIL_BQ)
            win = slice(qs, qs + DIL_KW)
            kpos = pos0 + qs - DIL_RADIUS + lax.broadcasted_iota(jnp.int32, (1, DIL_KW), 1)
            edge = jnp.where((kpos >= 0) & (kpos < seq_len), 0.0, NEG_INF).astype(F32)
            for h in range(N_HEADS):
                sl = _head_slice(h)
                s = lax.dot_general(q_ref[p, rows, sl], kext_ref[p, win, sl], NT_DIMS,
                                    preferred_element_type=F32)
                s = s + b_ref[0, h] + edge
                o, m, den = _softmax_pv([s], [vext_ref[p, win, sl]])
                o_ref[p, rows, sl] = o
                lse_ref[p, rows, h:h + 1] = m + jnp.log(den)


def dilated_group_attention(qkv, bias_table, *, group, name):
    d = D_MODEL
    dil = DIL_CONFIGS[group][1]
    assert DIL_CONFIGS[group][0] // (2 * dil) == DIL_RADIUS
    rows = N_TOK // dil
    tl = min(DIL_STEP_ROWS, DEC_SEQ // dil)
    planes = min(dil, DIL_STEP_ROWS // tl)
    assert (SEQ // dil) % tl == 0 and (DEC_SEQ // dil) % tl == 0 and tl % DIL_BQ == 0
    assert dil % planes == 0
    halo_per_tile = tl // DIL_RADIUS
    n_halo = rows // DIL_RADIUS

    def main_spec(part):
        return pl.BlockSpec((planes, tl, d), lambda r, t: (r, t, part))

    def prev_spec(part):
        return pl.BlockSpec((planes, DIL_RADIUS, d),
                            lambda r, t: (r, jnp.maximum(t * halo_per_tile - 1, 0), part))

    def next_spec(part):
        return pl.BlockSpec((planes, DIL_RADIUS, d),
                            lambda r, t: (r, jnp.minimum((t + 1) * halo_per_tile, n_halo - 1), part))

    return pl.pallas_call(
        functools.partial(_dil_body, dil=dil, tl=tl, planes=planes),
        grid=(dil // planes, rows // tl),
        in_specs=[main_spec(0), main_spec(1), main_spec(2),
                  prev_spec(1), next_spec(1), prev_spec(2), next_spec(2),
                  pl.BlockSpec((1, N_HEADS, DIL_BQ, DIL_KW), lambda r, t: (group, 0, 0, 0))],
        out_specs=[pl.BlockSpec((planes, tl, d), lambda r, t: (r, t, 0)),
                   pl.BlockSpec((planes, tl, HEAD_DIM), lambda r, t: (r, t, 0))],
        out_shape=[jax.ShapeDtypeStruct((dil, rows, d), F32),
                   jax.ShapeDtypeStruct((dil, rows, HEAD_DIM), F32)],
        scratch_shapes=[pltpu.VMEM((planes, tl + 2 * DIL_RADIUS, d), BF16),
                        pltpu.VMEM((planes, tl + 2 * DIL_RADIUS, d), BF16)],
        compiler_params=_params("parallel", "parallel"),
        name=name,
    )(*([qkv] * 7), bias_table)


def _t5_bucket(rel):
    nb = T5_BUCKETS // 2
    max_exact = nb // 2
    ret = jnp.where(rel > 0, nb, 0)
    n = jnp.abs(rel)
    n_f = jnp.maximum(n, 1).astype(F32)
    large = max_exact + (jnp.log(n_f / max_exact) / math.log(T5_MAX_DIST / max_exact)
                         * (nb - max_exact)).astype(jnp.int32)
    large = jnp.minimum(large, nb - 1)
    return ret + jnp.where(n < max_exact, n, large)


def dilated_bias_table(t5_table):
    period = DIL_BQ + DIL_KW
    rel = np.arange(-DIL_RADIUS, DIL_RADIUS + 1)
    pad = jnp.full((N_HEADS, period - rel.size), NEG_INF, F32)
    tables = []
    for g, (_, dil) in enumerate(DIL_CONFIGS):
        band = t5_table[:, g].astype(F32)[_t5_bucket(jnp.asarray(rel * dil, jnp.int32))]
        tables.append(_toeplitz(jnp.concatenate([band.T, pad], axis=-1), DIL_BQ, DIL_KW))
    return jnp.stack(tables)


def _merge_project_body(*refs, tm):
    o_refs = refs[:N_GROUPS]
    l_refs = refs[N_GROUPS:2 * N_GROUPS]
    w_ref, r_ref = refs[2 * N_GROUPS:2 * N_GROUPS + 2]
    cross_refs = refs[2 * N_GROUPS + 2:-4]
    out_ref, ltok_ref, otok_ref, a_ref = refs[-4:]
    dils = [dil for _, dil in DIL_CONFIGS]
    assert dils[0] == 1

    def to_token_order(dst_ref, idx, src, dil, r):
        if dil == 1:
            dst_ref[idx] = src
        else:
            dst_ref[idx + (pl.ds(r, tm // dil, stride=dil), slice(None))] = src

    for g, dil in enumerate(dils):
        for r in range(dil):
            to_token_order(ltok_ref, (g,), l_refs[g][r], dil, r)
    lses = [ltok_ref[g] for g in range(N_GROUPS)]
    top = functools.reduce(jnp.maximum, lses)
    wts = [jnp.exp(l - top) for l in lses]
    z = functools.reduce(jnp.add, wts)
    wts = [w / z for w in wts]

    for g, dil in enumerate(dils):
        if dil == 1:
            continue
        for r in range(dil):
            for h in range(N_HEADS):
                to_token_order(otok_ref, (g - 1, h), o_refs[g][r, :, _head_slice(h)], dil, r)
    for h in range(N_HEADS):
        sl = _head_slice(h)
        parts = [o_refs[0][0, :, sl]] + [otok_ref[g - 1, h] for g in range(1, N_GROUPS)]
        merged = functools.reduce(jnp.add, [p * wts[g][:, h:h + 1] for g, p in enumerate(parts)])
        a_ref[:, sl] = merged.astype(BF16)
    x = r_ref[...] + jnp.dot(a_ref[...], w_ref[...], preferred_element_type=F32)
    out_ref[...] = _cross_residual(x, *cross_refs)


def merge_project_cross_residual(outs, lses, w_stack, layer, res, cross_args, cross_layer, *, tm, name):
    m, d = res.shape
    dils = [dil for _, dil in DIL_CONFIGS]
    assert m % tm == 0 and all(tm % (8 * dil) == 0 for dil in dils)
    in_specs = [pl.BlockSpec((dil, tm // dil, d), lambda i: (0, i, 0)) for dil in dils]
    in_specs += [pl.BlockSpec((dil, tm // dil, HEAD_DIM), lambda i: (0, i, 0)) for dil in dils]
    in_specs += [_layer_spec((d, d), layer, lambda i: (0, 0)), pl.BlockSpec((tm, d), lambda i: (i, 0))]
    in_specs += _cross_specs(tm, cross_layer)
    return pl.pallas_call(
        functools.partial(_merge_project_body, tm=tm),
        grid=(m // tm,),
        in_specs=in_specs,
        out_specs=pl.BlockSpec((tm, d), lambda i: (i, 0)),
        out_shape=jax.ShapeDtypeStruct((m, d), F32),
        scratch_shapes=[pltpu.VMEM((N_GROUPS, tm, HEAD_DIM), F32),
                        pltpu.VMEM((N_GROUPS - 1, N_HEADS, tm, HEAD_DIM), F32),
                        pltpu.VMEM((tm, d), BF16)],
        compiler_params=_params("parallel"),
        name=name,
    )(*outs, *lses, w_stack, res, *cross_args)


def _qk_gains(q_gains, k_gains):
    tiled = [jnp.tile(g.astype(F32).reshape(-1, HEAD_DIM), (1, N_HEADS)) for g in (q_gains, k_gains)]
    return jnp.concatenate([tiled[0] * ATTN_SCALE, tiled[1]], axis=1)[:, None]


def kernel(x_prompt, x_sample, mem_prompt, mem_sample, g_mix, g_cross, g_mem, g_mlp, w_qkv_a, q_norm_a, k_norm_a, rpb_a, w_o_a, w_qkv_b, q_norm_b, k_norm_b, t5_table, w_o_b, w_q_x, w_kv_x, q_norm_x, k_norm_x, w_o_x, w_up, w_down):
    d = D_MODEL
    x = (x_prompt.reshape(SEQ, d), x_sample.reshape(DEC_BATCH * DEC_SEQ, d))
    mem = jnp.concatenate([mem_prompt.reshape(N_MEM, d), mem_sample.reshape(DEC_BATCH * N_MEM, d)])
    dil_bias = dilated_bias_table(t5_table)
    na_bias = na_bias_tables(rpb_a)
    (w_qkv_a, w_o_a, w_qkv_b, w_o_b, w_q_x, w_kv_x, w_o_x, w_up, w_down) = (
        w.astype(BF16) for w in (w_qkv_a, w_o_a, w_qkv_b, w_o_b, w_q_x, w_kv_x, w_o_x, w_up, w_down))
    g_mix, g_cross, g_mem, g_mlp = (g.astype(F32)[:, None] for g in (g_mix, g_cross, g_mem, g_mlp))
    qk_gain_a = _qk_gains(q_norm_a, k_norm_a)
    qk_gain_b = _qk_gains(q_norm_b, k_norm_b)
    k_gain_x = jnp.tile(k_norm_x.astype(F32), (1, X_HEADS))[:, None]
    q_gain_x = (q_norm_x.astype(F32) * ATTN_SCALE)[:, None]

    def cross_args(i):
        kv = norm_matmul(mem, g_mem, i, w_kv_x, i, k_gain_x, i,
                         n=2 * X_HEADS * HEAD_DIM, tm=N_SEQS * N_MEM, tn=X_HEADS * HEAD_DIM,
                         name=f"kv_x{i}").reshape(N_SEQS * N_MEM, 2 * X_HEADS * HEAD_DIM)
        return (g_cross, w_q_x, q_gain_x, kv, kv, w_o_x)

    for i in range(DEPTH):
        li = i // 2
        if i % 2 == 0:
            qkv = norm_matmul(x, g_mix, i, w_qkv_a, li, qk_gain_a, li,
                              n=3 * d, tm=1024, tn=1024 if isinstance(x, tuple) else 2048,
                              name=f"qkv_a{i}")
            o = neighborhood_attention(qkv.reshape(N_TOK, 3 * d), na_bias, li, name=f"na_attn{i}")
            x = project_cross_residual(o, w_o_a, li, x, cross_args(i), i,
                                       tm=256 if isinstance(x, tuple) else 512, name=f"wo_cross{i}")
        else:
            outs, lses = [], []
            for g, (_, dil) in enumerate(DIL_CONFIGS):
                qkv = norm_matmul(x, g_mix, i, w_qkv_b, li, qk_gain_b, li * N_GROUPS + g,
                                  n=3 * d, col0=g * 3 * d, tm=1024, tn=2048, dil=dil,
                                  name=f"qkv_b{i}_{g}")
                o_g, lse_g = dilated_group_attention(qkv, dil_bias, group=g, name=f"dil_attn{i}_{g}")
                outs.append(o_g)
                lses.append(lse_g)
            x = merge_project_cross_residual(outs, lses, w_o_b, li, x, cross_args(i), i,
                                             tm=256, name=f"wo_cross{i}")
        x = mlp_residual(x, g_mlp, w_up, w_down, i,
                         tm=512, tf=1024, split_groups=(i == DEPTH - 1), name=f"mlp{i}")

    y_prompt, y_sample = x
    return (y_prompt.reshape(1, SEQ, d), y_sample.reshape(DEC_BATCH, DEC_SEQ, d))
```

```python
import functools
import math

import numpy as np
import jax
import jax.numpy as jnp
from jax import lax
from jax.experimental import pallas as pl
from jax.experimental.pallas import tpu as pltpu

D_MODEL = 2048
SEQ = 8192
DEPTH = 4
DEC_BATCH = 4
DEC_SEQ = 2048
N_TOK = SEQ + DEC_BATCH * DEC_SEQ
N_SEQS = 1 + DEC_BATCH

HEAD_DIM = 128
N_HEADS = D_MODEL // HEAD_DIM
NA_WIN_ROWS = 8
NA_WIN_COLS = 16
GRID_W = 64
DIL_CONFIGS = ((128, 1), (512, 4), (2048, 16))
N_GROUPS = len(DIL_CONFIGS)
DIL_RADIUS = 64
T5_BUCKETS = 32
T5_MAX_DIST = 1024
X_HEADS = 4
N_MEM = 256
D_FF = 4 * D_MODEL
RMS_EPS = 1e-6
ATTN_SCALE = 1.0 / math.sqrt(HEAD_DIM)
NEG_INF = float("-inf")

F32 = jnp.float32
BF16 = jnp.bfloat16
NT_DIMS = (((1,), (1,)), ((), ()))

VMEM_LIMIT_BYTES = 56 * 1024 * 1024

NA_QROWS = 4
NA_BLK = NA_QROWS * GRID_W
NA_KBLKS = 3
NA_NBP = SEQ // NA_BLK
NA_NBS = DEC_SEQ // NA_BLK

DIL_BQ = 128
DIL_KW = DIL_BQ + 2 * DIL_RADIUS
DIL_STEP_ROWS = 512

MATMUL_CHUNK = 256
DEINTERLEAVE_STRIDE = 4
SLAB_RING = 4


def _params(*sem):
    return pltpu.CompilerParams(dimension_semantics=sem, vmem_limit_bytes=VMEM_LIMIT_BYTES)


def _layer_spec(block, layer, index_map):
    return pl.BlockSpec((None,) + block, lambda *ids: (layer,) + index_map(*ids))


def _group_row_specs(tm, width):
    tp = SEQ // tm
    return tp, [pl.BlockSpec((tm, width), lambda i, *_: (jnp.minimum(i, tp - 1), 0)),
                pl.BlockSpec((tm, width), lambda i, *_: (jnp.maximum(i - tp, 0), 0))]


def _norm_rows(x, g):
    ms = jnp.mean(x * x, axis=-1, keepdims=True)
    return x * lax.rsqrt(ms + RMS_EPS) * g


def _head_slice(h):
    return slice(h * HEAD_DIM, (h + 1) * HEAD_DIM)


def _softmax_pv(logits, values, ones=None):
    m = functools.reduce(jnp.maximum, [jnp.max(s, axis=-1, keepdims=True) for s in logits])
    if ones is None:
        probs = [jnp.exp(s - m) for s in logits]
        den = functools.reduce(jnp.add, [jnp.sum(p, axis=-1, keepdims=True) for p in probs])
        acc = functools.reduce(jnp.add, [jnp.dot(p.astype(BF16), v, preferred_element_type=F32)
                                         for p, v in zip(probs, values)])
        return acc / den, m, den
    res = functools.reduce(jnp.add, [
        jnp.dot(jnp.exp((s - m).astype(BF16)), jnp.concatenate([v, ones], axis=1),
                preferred_element_type=F32)
        for s, v in zip(logits, values)])
    den = res[:, HEAD_DIM:]
    return res[:, :HEAD_DIM] / den, m, den[:, :1]


def _norm_matmul_body(*refs, n_norm_blocks, n_blocks, tm, tn, dil, prompt_tiles):
    n_x = 1 if prompt_tiles is None else 2
    x_refs = refs[:n_x]
    g_ref, w_ref, hg_ref, o_ref, xn_ref = refs[n_x:n_x + 5]
    slab_refs = refs[n_x + 5:]
    i = pl.program_id(0)
    j = pl.program_id(1)

    def normalise(x_ref):
        xn_ref[...] = _norm_rows(x_ref[...], g_ref[...]).astype(BF16)

    if prompt_tiles is None:
        pl.when(j == 0)(functools.partial(normalise, x_refs[0]))
    else:
        pl.when((j == 0) & (i < prompt_tiles))(functools.partial(normalise, x_refs[0]))
        pl.when((j == 0) & (i >= prompt_tiles))(functools.partial(normalise, x_refs[1]))

    def run(head_norm):
        for c in range(tn // MATMUL_CHUNK):
            acc = jnp.dot(xn_ref[...], w_ref[:, c * MATMUL_CHUNK:(c + 1) * MATMUL_CHUNK],
                          preferred_element_type=F32)
            for s in range(MATMUL_CHUNK // HEAD_DIM):
                slab = c * (MATMUL_CHUNK // HEAD_DIM) + s
                sl = _head_slice(slab)
                piece = acc[:, _head_slice(s)]
                if head_norm:
                    piece = _norm_rows(piece, hg_ref[:, sl])
                if dil == 1:
                    o_ref[0, :, sl] = piece.astype(o_ref.dtype)
                    continue
                slab_ref = slab_refs[0]
                slot = slab % SLAB_RING
                slab_ref[slot] = piece
                if dil == DEINTERLEAVE_STRIDE:
                    for r in range(dil):
                        rows = slab_ref[slot, pl.ds(r, tm // dil, stride=dil), :]
                        o_ref[r, :, sl] = rows.astype(o_ref.dtype)
                else:
                    quarter_ref = slab_refs[1]
                    outer = dil // DEINTERLEAVE_STRIDE
                    for b in range(DEINTERLEAVE_STRIDE):
                        quarter_ref[slot, b] = slab_ref[
                            slot, pl.ds(b, tm // DEINTERLEAVE_STRIDE, stride=DEINTERLEAVE_STRIDE), :]
                        for a in range(outer):
                            rows = quarter_ref[slot, b, pl.ds(a, tm // dil, stride=outer), :]
                            o_ref[a * DEINTERLEAVE_STRIDE + b, :, sl] = rows.astype(o_ref.dtype)

    if n_norm_blocks == n_blocks:
        run(True)
    else:
        pl.when(j < n_norm_blocks)(functools.partial(run, True))
        pl.when(j >= n_norm_blocks)(functools.partial(run, False))


def norm_matmul(x, gains, gain_idx, w_stack, layer, head_gains, head_gain_idx, *, n, tm, tn, name,
                col0=0, dil=1):
    pair = isinstance(x, tuple)
    m = sum(part.shape[0] for part in x) if pair else x.shape[0]
    k = w_stack.shape[1]
    n_norm = head_gains.shape[2]
    n_blocks = n // tn
    n_norm_blocks = n_norm // tn
    assert m % tm == 0 and n % tn == 0 and n_norm % tn == 0 and n_norm_blocks >= 1 and col0 % tn == 0
    assert tn % MATMUL_CHUNK == 0 and tm % (16 * dil) == 0
    if pair:
        prompt_tiles, x_specs = _group_row_specs(tm, k)
        x_args = list(x)
    else:
        prompt_tiles, x_specs, x_args = None, [pl.BlockSpec((tm, k), lambda i, j: (i, 0))], [x]
    body = functools.partial(_norm_matmul_body, n_norm_blocks=n_norm_blocks, n_blocks=n_blocks,
                             tm=tm, tn=tn, dil=dil, prompt_tiles=prompt_tiles)
    scratch = [pltpu.VMEM((tm, k), BF16)]
    if dil > 1:
        assert dil % DEINTERLEAVE_STRIDE == 0 and dil // DEINTERLEAVE_STRIDE <= DEINTERLEAVE_STRIDE
        scratch.append(pltpu.VMEM((SLAB_RING, tm, HEAD_DIM), F32))
    if dil > DEINTERLEAVE_STRIDE:
        scratch.append(pltpu.VMEM((SLAB_RING, DEINTERLEAVE_STRIDE, tm // DEINTERLEAVE_STRIDE,
                                   HEAD_DIM), F32))
    return pl.pallas_call(
        body,
        grid=(m // tm, n_blocks),
        in_specs=x_specs + [
            _layer_spec((1, k), gain_idx, lambda i, j: (0, 0)),
            _layer_spec((k, tn), layer, lambda i, j: (0, col0 // tn + j)),
            _layer_spec((1, tn), head_gain_idx, lambda i, j: (0, jnp.minimum(j, n_norm_blocks - 1))),
        ],
        out_specs=pl.BlockSpec((dil, tm // dil, tn), lambda i, j: (0, i, j)),
        out_shape=jax.ShapeDtypeStruct((dil, m // dil, n), BF16),
        scratch_shapes=scratch,
        compiler_params=_params("arbitrary" if pair else "parallel", "arbitrary"),
        name=name,
    )(*x_args, gains, w_stack, head_gains)


def _project_cross_body(a_ref, w_ref, *refs, prompt_tiles):
    o_ref = refs[-1]
    if prompt_tiles is None:
        res, cross_refs = refs[0][...], refs[1:-1]
    else:
        res = jnp.where(pl.program_id(0) < prompt_tiles, refs[0][...], refs[1][...])
        cross_refs = refs[2:-1]
    x = res + jnp.dot(a_ref[...], w_ref[...], preferred_element_type=F32)
    o_ref[...] = _cross_residual(x, *cross_refs)


def project_cross_residual(a, w_stack, layer, res, cross_args, cross_layer, *, tm, name):
    m, k = a.shape
    n = w_stack.shape[2]
    assert m % tm == 0
    if isinstance(res, tuple):
        prompt_tiles, res_specs = _group_row_specs(tm, n)
        res_args = list(res)
    else:
        prompt_tiles, res_specs, res_args = None, [pl.BlockSpec((tm, n), lambda i: (i, 0))], [res]
    return pl.pallas_call(
        functools.partial(_project_cross_body, prompt_tiles=prompt_tiles),
        grid=(m // tm,),
        in_specs=[pl.BlockSpec((tm, k), lambda i: (i, 0)),
                  _layer_spec((k, n), layer, lambda i: (0, 0))] + res_specs
        + _cross_specs(tm, cross_layer),
        out_specs=pl.BlockSpec((tm, n), lambda i: (i, 0)),
        out_shape=jax.ShapeDtypeStruct((m, n), F32),
        compiler_params=_params("arbitrary" if prompt_tiles is not None else "parallel"),
        name=name,
    )(a, w_stack, *res_args, *cross_args)


def _mlp_body(x_ref, g_ref, wu_ref, wd_ref, *refs, prompt_tiles):
    xn_ref = refs[-1]

    def step(o_ref):
        @pl.when(pl.program_id(1) == 0)
        def _():
            x = x_ref[...]
            xn_ref[...] = _norm_rows(x, g_ref[...]).astype(BF16)
            o_ref[...] = x

        h = jnp.dot(xn_ref[...], wu_ref[...], preferred_element_type=F32)
        h = jnp.maximum(h, 0.0)
        o_ref[...] += jnp.dot((h * h).astype(BF16), wd_ref[...], preferred_element_type=F32)

    if prompt_tiles is None:
        step(refs[0])
    else:
        i = pl.program_id(0)
        pl.when(i < prompt_tiles)(functools.partial(step, refs[0]))
        pl.when(i >= prompt_tiles)(functools.partial(step, refs[1]))


def mlp_residual(x, gains, w_up, w_down, layer, *, tm, tf, name, split_groups=False):
    m, d = x.shape
    ff = w_up.shape[2]
    assert m % tm == 0 and ff % tf == 0 and SEQ % tm == 0
    if split_groups:
        tp = SEQ // tm
        out_specs = [pl.BlockSpec((tm, d), lambda i, f: (jnp.minimum(i, tp - 1), 0)),
                     pl.BlockSpec((tm, d), lambda i, f: (jnp.maximum(i - tp, 0), 0))]
        out_shape = [jax.ShapeDtypeStruct((SEQ, d), F32), jax.ShapeDtypeStruct((m - SEQ, d), F32)]
    else:
        tp = None
        out_specs = pl.BlockSpec((tm, d), lambda i, f: (i, 0))
        out_shape = jax.ShapeDtypeStruct((m, d), F32)
    return pl.pallas_call(
        functools.partial(_mlp_body, prompt_tiles=tp),
        grid=(m // tm, ff // tf),
        in_specs=[
            pl.BlockSpec((tm, d), lambda i, f: (i, 0)),
            _layer_spec((1, d), layer, lambda i, f: (0, 0)),
            _layer_spec((d, tf), layer, lambda i, f: (0, f)),
            _layer_spec((tf, d), layer, lambda i, f: (f, 0)),
        ],
        out_specs=out_specs,
        out_shape=out_shape,
        scratch_shapes=[pltpu.VMEM((tm, d), BF16)],
        compiler_params=_params("arbitrary" if split_groups else "parallel", "arbitrary"),
        name=name,
    )(x, gains, w_up, w_down)


def _cross_residual(x, g_ref, wq_ref, qn_ref, k_ref, v_ref, wo_ref):
    xn = _norm_rows(x, g_ref[...]).astype(BF16)
    q = jnp.dot(xn, wq_ref[...], preferred_element_type=F32)
    heads = []
    for h in range(X_HEADS):
        sl = _head_slice(h)
        qh = _norm_rows(q[:, sl], qn_ref[...]).astype(BF16)
        s = lax.dot_general(qh, k_ref[:, sl], NT_DIMS, preferred_element_type=F32)
        oh, _, _ = _softmax_pv([s], [v_ref[:, sl]])
        heads.append(oh.astype(BF16))
    o = jnp.concatenate(heads, axis=-1)
    return x + jnp.dot(o, wo_ref[...], preferred_element_type=F32)


def _cross_specs(tm, layer):
    d = D_MODEL
    dx = X_HEADS * HEAD_DIM
    assert SEQ % tm == 0 and DEC_SEQ % tm == 0
    tiles_p = SEQ // tm
    tiles_s = DEC_SEQ // tm

    def mem_seq(i):
        return jnp.where(i < tiles_p, 0, 1 + (i - tiles_p) // tiles_s)

    return [
        _layer_spec((1, d), layer, lambda i: (0, 0)),
        _layer_spec((d, dx), layer, lambda i: (0, 0)),
        _layer_spec((1, HEAD_DIM), layer, lambda i: (0, 0)),
        pl.BlockSpec((N_MEM, dx), lambda i: (mem_seq(i), 0)),
        pl.BlockSpec((N_MEM, dx), lambda i: (mem_seq(i), 1)),
        _layer_spec((dx, d), layer, lambda i: (0, 0)),
    ]


def _na_block_info(i):
    in_prompt = i < NA_NBP
    local = jnp.where(in_prompt, i, (i - NA_NBP) % NA_NBS)
    nblk = jnp.where(in_prompt, NA_NBP, NA_NBS)
    base = jnp.where(in_prompt, 0, NA_NBP + ((i - NA_NBP) // NA_NBS) * NA_NBS)
    kblk = base + jnp.clip(local - 1, 0, nblk - NA_KBLKS)
    placement = jnp.where(local == 0, 0, jnp.where(local == nblk - 1, 2, 1))
    return kblk, placement


def _na_body(q_ref, k0_ref, k1_ref, k2_ref, v0_ref, v1_ref, v2_ref, b_ref, o_ref):
    k_refs = (k0_ref, k1_ref, k2_ref)
    v_refs = (v0_ref, v1_ref, v2_ref)
    ones = jnp.ones((NA_BLK, HEAD_DIM), BF16)
    for h in range(N_HEADS):
        sl = _head_slice(h)
        q = q_ref[:, sl]
        logits = []
        for c in range(NA_KBLKS):
            s = lax.dot_general(q, k_refs[c][:, sl], NT_DIMS, preferred_element_type=F32)
            logits.append(s + b_ref[0, h, :, c * NA_BLK:(c + 1) * NA_BLK])
        o, _, _ = _softmax_pv(logits, [v_ref[:, sl] for v_ref in v_refs], ones)
        o_ref[:, sl] = o.astype(o_ref.dtype)


def neighborhood_attention(qkv, bias_tables, layer, *, name):
    d = D_MODEL

    def kv_spec(c, part):
        return pl.BlockSpec((NA_BLK, d), lambda i: (_na_block_info(i)[0] + c, part))

    return pl.pallas_call(
        _na_body,
        grid=(N_TOK // NA_BLK,),
        in_specs=[pl.BlockSpec((NA_BLK, d), lambda i: (i, 0))]
        + [kv_spec(c, 1) for c in range(NA_KBLKS)]
        + [kv_spec(c, 2) for c in range(NA_KBLKS)]
        + [pl.BlockSpec((None, 1, N_HEADS, NA_BLK, NA_KBLKS * NA_BLK),
                        lambda i: (layer, _na_block_info(i)[1], 0, 0, 0))],
        out_specs=pl.BlockSpec((NA_BLK, d), lambda i: (i, 0)),
        out_shape=jax.ShapeDtypeStruct((N_TOK, d), BF16),
        compiler_params=_params("parallel"),
        name=name,
    )(*([qkv] * (1 + 2 * NA_KBLKS)), bias_tables)


def _toeplitz(v, n_rows, n_cols):
    period = v.shape[-1]
    assert period >= n_rows + n_cols - 1
    flat = jnp.tile(v, (1,) * (v.ndim - 1) + (n_rows,))[..., :n_rows * (period - 1)]
    return flat.reshape(v.shape[:-1] + (n_rows, period - 1))[..., :n_cols]


def na_bias_tables(rpb):
    n_layers = rpb.shape[0]
    n_krows = NA_KBLKS * NA_QROWS
    n_dr = 2 * NA_WIN_ROWS - 1
    assert n_krows - NA_WIN_ROWS == NA_QROWS
    c = np.arange(GRID_W)
    cs = np.clip(c - NA_WIN_COLS // 2, 0, GRID_W - NA_WIN_COLS)
    col_ok = (c[None, :] >= cs[:, None]) & (c[None, :] < cs[:, None] + NA_WIN_COLS)
    dist = np.arange(2 * GRID_W)
    dist = np.where(dist < GRID_W, dist, dist - 2 * GRID_W)
    by_dist = rpb.astype(F32)[..., np.clip(dist, -(NA_WIN_COLS - 1), NA_WIN_COLS - 1) + (NA_WIN_COLS - 1)]
    tiles = jnp.where(col_ok, _toeplitz(by_dist, GRID_W, GRID_W), NEG_INF)
    side_by_side = tiles.transpose(0, 1, 3, 2, 4).reshape(n_layers, N_HEADS, GRID_W, n_dr * GRID_W)

    def strip(first_dr, lead):
        window = side_by_side[..., first_dr * GRID_W:(first_dr + NA_WIN_ROWS) * GRID_W]
        pad = [(0, 0)] * 3 + [(lead * GRID_W, (n_krows - NA_WIN_ROWS - lead) * GRID_W)]
        return jnp.pad(window, pad, constant_values=NEG_INF)

    placements = (
        [strip(7 - rq, 0) for rq in range(NA_QROWS)],
        [strip(3, rq) for rq in range(NA_QROWS)],
        [strip(3 - rq, 4) for rq in range(NA_QROWS)],
    )
    return jnp.stack([jnp.concatenate(strips, axis=2) for strips in placements], axis=1)


def _dil_body(q_ref, k_ref, v_ref, kp_ref, kn_ref, vp_ref, vn_ref, b_ref, o_ref, lse_ref,
              kext_ref, vext_ref, *, dil, tl, planes):
    lse_ref[...] = jnp.zeros_like(lse_ref)

    row0 = pl.program_id(1) * tl
    rows_prompt = SEQ // dil
    len_sample = DEC_SEQ // dil
    in_prompt = row0 < rows_prompt
    pos0 = jnp.where(in_prompt, row0, (row0 - rows_prompt) % len_sample)
    seq_len = jnp.where(in_prompt, rows_prompt, len_sample)

    for p in range(planes):
        for ext_ref, prev_ref, cur_ref, next_ref in ((kext_ref, kp_ref, k_ref, kn_ref),
                                                     (vext_ref, vp_ref, v_ref, vn_ref)):
            ext_ref[p, 0:DIL_RADIUS] = prev_ref[p]
            ext_ref[p, DIL_RADIUS:DIL_RADIUS + tl] = cur_ref[p]
            ext_ref[p, DIL_RADIUS + tl:] = next_ref[p]

        for qs in range(0, tl, DIL_BQ):
            rows = slice(qs, qs + DIL_BQ)
            win = slice(qs, qs + DIL_KW)
            kpos = pos0 + qs - DIL_RADIUS + lax.broadcasted_iota(jnp.int32, (1, DIL_KW), 1)
            edge = jnp.where((kpos >= 0) & (kpos < seq_len), 0.0, NEG_INF).astype(F32)
            for h in range(N_HEADS):
                sl = _head_slice(h)
                s = lax.dot_general(q_ref[p, rows, sl], kext_ref[p, win, sl], NT_DIMS,
                                    preferred_element_type=F32)
                s = s + b_ref[0, h] + edge
                o, m, den = _softmax_pv([s], [vext_ref[p, win, sl]])
                o_ref[p, rows, sl] = o
                lse_ref[p, rows, h:h + 1] = m + jnp.log(den)


def dilated_group_attention(qkv, bias_table, *, group, name):
    d = D_MODEL
    dil = DIL_CONFIGS[group][1]
    assert DIL_CONFIGS[group][0] // (2 * dil) == DIL_RADIUS
    rows = N_TOK // dil
    tl = min(DIL_STEP_ROWS, DEC_SEQ // dil)
    planes = min(dil, DIL_STEP_ROWS // tl)
    assert (SEQ // dil) % tl == 0 and (DEC_SEQ // dil) % tl == 0 and tl % DIL_BQ == 0
    assert dil % planes == 0
    halo_per_tile = tl // DIL_RADIUS
    n_halo = rows // DIL_RADIUS

    def main_spec(part):
        return pl.BlockSpec((planes, tl, d), lambda r, t: (r, t, part))

    def prev_spec(part):
        return pl.BlockSpec((planes, DIL_RADIUS, d),
                            lambda r, t: (r, jnp.maximum(t * halo_per_tile - 1, 0), part))

    def next_spec(part):
        return pl.BlockSpec((planes, DIL_RADIUS, d),
                            lambda r, t: (r, jnp.minimum((t + 1) * halo_per_tile, n_halo - 1), part))

    return pl.pallas_call(
        functools.partial(_dil_body, dil=dil, tl=tl, planes=planes),
        grid=(dil // planes, rows // tl),
        in_specs=[main_spec(0), main_spec(1), main_spec(2),
                  prev_spec(1), next_spec(1), prev_spec(2), next_spec(2),
                  pl.BlockSpec((1, N_HEADS, DIL_BQ, DIL_KW), lambda r, t: (group, 0, 0, 0))],
        out_specs=[pl.BlockSpec((planes, tl, d), lambda r, t: (r, t, 0)),
                   pl.BlockSpec((planes, tl, HEAD_DIM), lambda r, t: (r, t, 0))],
        out_shape=[jax.ShapeDtypeStruct((dil, rows, d), F32),
                   jax.ShapeDtypeStruct((dil, rows, HEAD_DIM), F32)],
        scratch_shapes=[pltpu.VMEM((planes, tl + 2 * DIL_RADIUS, d), BF16),
                        pltpu.VMEM((planes, tl + 2 * DIL_RADIUS, d), BF16)],
        compiler_params=_params("parallel", "parallel"),
        name=name,
    )(*([qkv] * 7), bias_table)


def _t5_bucket(rel):
    nb = T5_BUCKETS // 2
    max_exact = nb // 2
    ret = jnp.where(rel > 0, nb, 0)
    n = jnp.abs(rel)
    n_f = jnp.maximum(n, 1).astype(F32)
    large = max_exact + (jnp.log(n_f / max_exact) / math.log(T5_MAX_DIST / max_exact)
                         * (nb - max_exact)).astype(jnp.int32)
    large = jnp.minimum(large, nb - 1)
    return ret + jnp.where(n < max_exact, n, large)


def dilated_bias_table(t5_table):
    period = DIL_BQ + DIL_KW
    rel = np.arange(-DIL_RADIUS, DIL_RADIUS + 1)
    pad = jnp.full((N_HEADS, period - rel.size), NEG_INF, F32)
    tables = []
    for g, (_, dil) in enumerate(DIL_CONFIGS):
        band = t5_table[:, g].astype(F32)[_t5_bucket(jnp.asarray(rel * dil, jnp.int32))]
        tables.append(_toeplitz(jnp.concatenate([band.T, pad], axis=-1), DIL_BQ, DIL_KW))
    return jnp.stack(tables)


def _merge_project_body(*refs, tm):
    o_refs = refs[:N_GROUPS]
    l_refs = refs[N_GROUPS:2 * N_GROUPS]
    w_ref, r_ref = refs[2 * N_GROUPS:2 * N_GROUPS + 2]
    cross_refs = refs[2 * N_GROUPS + 2:-4]
    out_ref, ltok_ref, otok_ref, a_ref = refs[-4:]
    dils = [dil for _, dil in DIL_CONFIGS]
    assert dils[0] == 1

    def to_token_order(dst_ref, idx, src, dil, r):
        if dil == 1:
            dst_ref[idx] = src
        else:
            dst_ref[idx + (pl.ds(r, tm // dil, stride=dil), slice(None))] = src

    for g, dil in enumerate(dils):
        for r in range(dil):
            to_token_order(ltok_ref, (g,), l_refs[g][r], dil, r)
    lses = [ltok_ref[g] for g in range(N_GROUPS)]
    top = functools.reduce(jnp.maximum, lses)
    wts = [jnp.exp(l - top) for l in lses]
    z = functools.reduce(jnp.add, wts)
    wts = [w / z for w in wts]

    for g, dil in enumerate(dils):
        if dil == 1:
            continue
        for r in range(dil):
            for h in range(N_HEADS):
                to_token_order(otok_ref, (g - 1, h), o_refs[g][r, :, _head_slice(h)], dil, r)
    for h in range(N_HEADS):
        sl = _head_slice(h)
        parts = [o_refs[0][0, :, sl]] + [otok_ref[g - 1, h] for g in range(1, N_GROUPS)]
        merged = functools.reduce(jnp.add, [p * wts[g][:, h:h + 1] for g, p in enumerate(parts)])
        a_ref[:, sl] = merged.astype(BF16)
    x = r_ref[...] + jnp.dot(a_ref[...], w_ref[...], preferred_element_type=F32)
    out_ref[...] = _cross_residual(x, *cross_refs)


def merge_project_cross_residual(outs, lses, w_stack, layer, res, cross_args, cross_layer, *, tm, name):
    m, d = res.shape
    dils = [dil for _, dil in DIL_CONFIGS]
    assert m % tm == 0 and all(tm % (8 * dil) == 0 for dil in dils)
    in_specs = [pl.BlockSpec((dil, tm // dil, d), lambda i: (0, i, 0)) for dil in dils]
    in_specs += [pl.BlockSpec((dil, tm // dil, HEAD_DIM), lambda i: (0, i, 0)) for dil in dils]
    in_specs += [_layer_spec((d, d), layer, lambda i: (0, 0)), pl.BlockSpec((tm, d), lambda i: (i, 0))]
    in_specs += _cross_specs(tm, cross_layer)
    return pl.pallas_call(
        functools.partial(_merge_project_body, tm=tm),
        grid=(m // tm,),
        in_specs=in_specs,
        out_specs=pl.BlockSpec((tm, d), lambda i: (i, 0)),
        out_shape=jax.ShapeDtypeStruct((m, d), F32),
        scratch_shapes=[pltpu.VMEM((N_GROUPS, tm, HEAD_DIM), F32),
                        pltpu.VMEM((N_GROUPS - 1, N_HEADS, tm, HEAD_DIM), F32),
                        pltpu.VMEM((tm, d), BF16)],
        compiler_params=_params("parallel"),
        name=name,
    )(*outs, *lses, w_stack, res, *cross_args)


def _qk_gains(q_gains, k_gains):
    tiled = [jnp.tile(g.astype(F32).reshape(-1, HEAD_DIM), (1, N_HEADS)) for g in (q_gains, k_gains)]
    return jnp.concatenate([tiled[0] * ATTN_SCALE, tiled[1]], axis=1)[:, None]


def kernel(x_prompt, x_sample, mem_prompt, mem_sample, g_mix, g_cross, g_mem, g_mlp, w_qkv_a, q_norm_a, k_norm_a, rpb_a, w_o_a, w_qkv_b, q_norm_b, k_norm_b, t5_table, w_o_b, w_q_x, w_kv_x, q_norm_x, k_norm_x, w_o_x, w_up, w_down):
    d = D_MODEL
    x = (x_prompt.reshape(SEQ, d), x_sample.reshape(DEC_BATCH * DEC_SEQ, d))
    mem = jnp.concatenate([mem_prompt.reshape(N_MEM, d), mem_sample.reshape(DEC_BATCH * N_MEM, d)])
    dil_bias = dilated_bias_table(t5_table)
    na_bias = na_bias_tables(rpb_a)
    (w_qkv_a, w_o_a, w_qkv_b, w_o_b, w_q_x, w_kv_x, w_o_x, w_up, w_down) = (
        w.astype(BF16) for w in (w_qkv_a, w_o_a, w_qkv_b, w_o_b, w_q_x, w_kv_x, w_o_x, w_up, w_down))
    g_mix, g_cross, g_mem, g_mlp = (g.astype(F32)[:, None] for g in (g_mix, g_cross, g_mem, g_mlp))
    qk_gain_a = _qk_gains(q_norm_a, k_norm_a)
    qk_gain_b = _qk_gains(q_norm_b, k_norm_b)
    k_gain_x = jnp.tile(k_norm_x.astype(F32), (1, X_HEADS))[:, None]
    q_gain_x = (q_norm_x.astype(F32) * ATTN_SCALE)[:, None]

    def cross_args(i):
        kv = norm_matmul(mem, g_mem, i, w_kv_x, i, k_gain_x, i,
                         n=2 * X_HEADS * HEAD_DIM, tm=N_SEQS * N_MEM, tn=X_HEADS * HEAD_DIM,
                         name=f"kv_x{i}").reshape(N_SEQS * N_MEM, 2 * X_HEADS * HEAD_DIM)
        return (g_cross, w_q_x, q_gain_x, kv, kv, w_o_x)

    for i in range(DEPTH):
        li = i // 2
        if i % 2 == 0:
            qkv = norm_matmul(x, g_mix, i, w_qkv_a, li, qk_gain_a, li,
                              n=3 * d, tm=1024, tn=1024 if isinstance(x, tuple) else 2048,
                              name=f"qkv_a{i}")
            o = neighborhood_attention(qkv.reshape(N_TOK, 3 * d), na_bias, li, name=f"na_attn{i}")
            x = project_cross_residual(o, w_o_a, li, x, cross_args(i), i,
                                       tm=256 if isinstance(x, tuple) else 512, name=f"wo_cross{i}")
        else:
            outs, lses = [], []
            for g, (_, dil) in enumerate(DIL_CONFIGS):
                qkv = norm_matmul(x, g_mix, i, w_qkv_b, li, qk_gain_b, li * N_GROUPS + g,
                                  n=3 * d, col0=g * 3 * d, tm=1024, tn=2048, dil=dil,
                                  name=f"qkv_b{i}_{g}")
                o_g, lse_g = dilated_group_attention(qkv, dil_bias, group=g, name=f"dil_attn{i}_{g}")
                outs.append(o_g)
                lses.append(lse_g)
            x = merge_project_cross_residual(outs, lses, w_o_b, li, x, cross_args(i), i,
                                             tm=256, name=f"wo_cross{i}")
        last = i == DEPTH - 1
        x = mlp_residual(x, g_mlp, w_up, w_down, i, tm=512 if last else 1024,
                         tf=1024 if last else 512, split_groups=last, name=f"mlp{i}")

    y_prompt, y_sample = x
    return (y_prompt.reshape(1, SEQ, d), y_sample.reshape(DEC_BATCH, DEC_SEQ, d))
```

```python
import functools
import math

import numpy as np
import jax
import jax.numpy as jnp
from jax import lax
from jax.experimental import pallas as pl
from jax.experimental.pallas import tpu as pltpu

D_MODEL = 2048
SEQ = 8192
DEPTH = 4
DEC_BATCH = 4
DEC_SEQ = 2048
N_TOK = SEQ + DEC_BATCH * DEC_SEQ
N_SEQS = 1 + DEC_BATCH

HEAD_DIM = 128
N_HEADS = D_MODEL // HEAD_DIM
NA_WIN_ROWS = 8
NA_WIN_COLS = 16
GRID_W = 64
DIL_CONFIGS = ((128, 1), (512, 4), (2048, 16))
N_GROUPS = len(DIL_CONFIGS)
DIL_RADIUS = 64
T5_BUCKETS = 32
T5_MAX_DIST = 1024
X_HEADS = 4
N_MEM = 256
D_FF = 4 * D_MODEL
RMS_EPS = 1e-6
ATTN_SCALE = 1.0 / math.sqrt(HEAD_DIM)
NEG_INF = float("-inf")

F32 = jnp.float32
BF16 = jnp.bfloat16
NT_DIMS = (((1,), (1,)), ((), ()))

VMEM_LIMIT_BYTES = 56 * 1024 * 1024

NA_QROWS = 4
NA_BLK = NA_QROWS * GRID_W
NA_KBLKS = 3
NA_NBP = SEQ // NA_BLK
NA_NBS = DEC_SEQ // NA_BLK

DIL_BQ = 128
DIL_KW = DIL_BQ + 2 * DIL_RADIUS
DIL_STEP_ROWS = 512
DIL_UNROLL = 2

MATMUL_CHUNK = 256
DEINTERLEAVE_STRIDE = 4
SLAB_RING = 4


def _params(*sem):
    return pltpu.CompilerParams(dimension_semantics=sem, vmem_limit_bytes=VMEM_LIMIT_BYTES)


def _layer_spec(block, layer, index_map, resident=False):
    return pl.BlockSpec((None,) + block, lambda *ids: (layer,) + index_map(*ids),
                        pipeline_mode=pl.Buffered(1) if resident else None)


def _group_row_specs(tm, width):
    tp = SEQ // tm
    return tp, [pl.BlockSpec((tm, width), lambda i, *_: (jnp.minimum(i, tp - 1), 0)),
                pl.BlockSpec((tm, width), lambda i, *_: (jnp.maximum(i - tp, 0), 0))]


def _norm_rows(x, g):
    ms = jnp.mean(x * x, axis=-1, keepdims=True)
    return x * lax.rsqrt(ms + RMS_EPS) * g


def _head_slice(h):
    return slice(h * HEAD_DIM, (h + 1) * HEAD_DIM)


def _softmax_pv(logits, values, ones=None):
    m = functools.reduce(jnp.maximum, [jnp.max(s, axis=-1, keepdims=True) for s in logits])
    if ones is None:
        probs = [jnp.exp(s - m) for s in logits]
        den = functools.reduce(jnp.add, [jnp.sum(p, axis=-1, keepdims=True) for p in probs])
        acc = functools.reduce(jnp.add, [jnp.dot(p.astype(BF16), v, preferred_element_type=F32)
                                         for p, v in zip(probs, values)])
        return acc / den, m, den
    res = functools.reduce(jnp.add, [
        jnp.dot(jnp.exp((s - m).astype(BF16)), jnp.concatenate([v, ones], axis=1),
                preferred_element_type=F32)
        for s, v in zip(logits, values)])
    den = res[:, HEAD_DIM:]
    return res[:, :HEAD_DIM] / den, m, den[:, :1]


def _norm_matmul_body(*refs, n_norm_blocks, n_blocks, tm, tn, dil, prompt_tiles):
    n_x = 1 if prompt_tiles is None else 2
    x_refs = refs[:n_x]
    g_ref, w_ref, hg_ref, o_ref, xn_ref = refs[n_x:n_x + 5]
    slab_refs = refs[n_x + 5:]
    i = pl.program_id(0)
    j = pl.program_id(1)

    def normalise(x_ref):
        xn_ref[...] = _norm_rows(x_ref[...], g_ref[...]).astype(BF16)

    if prompt_tiles is None:
        pl.when(j == 0)(functools.partial(normalise, x_refs[0]))
    else:
        pl.when((j == 0) & (i < prompt_tiles))(functools.partial(normalise, x_refs[0]))
        pl.when((j == 0) & (i >= prompt_tiles))(functools.partial(normalise, x_refs[1]))

    def run(head_norm):
        for c in range(tn // MATMUL_CHUNK):
            acc = jnp.dot(xn_ref[...], w_ref[:, c * MATMUL_CHUNK:(c + 1) * MATMUL_CHUNK],
                          preferred_element_type=F32)
            for s in range(MATMUL_CHUNK // HEAD_DIM):
                slab = c * (MATMUL_CHUNK // HEAD_DIM) + s
                sl = _head_slice(slab)
                piece = acc[:, _head_slice(s)]
                if head_norm:
                    piece = _norm_rows(piece, hg_ref[:, sl])
                if dil == 1:
                    o_ref[0, :, sl] = piece.astype(o_ref.dtype)
                    continue
                slab_ref = slab_refs[0]
                slot = slab % SLAB_RING
                slab_ref[slot] = piece
                if dil == DEINTERLEAVE_STRIDE:
                    for r in range(dil):
                        rows = slab_ref[slot, pl.ds(r, tm // dil, stride=dil), :]
                        o_ref[r, :, sl] = rows.astype(o_ref.dtype)
                else:
                    quarter_ref = slab_refs[1]
                    outer = dil // DEINTERLEAVE_STRIDE
                    for b in range(DEINTERLEAVE_STRIDE):
                        quarter_ref[slot, b] = slab_ref[
                            slot, pl.ds(b, tm // DEINTERLEAVE_STRIDE, stride=DEINTERLEAVE_STRIDE), :]
                        for a in range(outer):
                            rows = quarter_ref[slot, b, pl.ds(a, tm // dil, stride=outer), :]
                            o_ref[a * DEINTERLEAVE_STRIDE + b, :, sl] = rows.astype(o_ref.dtype)

    if n_norm_blocks == n_blocks:
        run(True)
    else:
        pl.when(j < n_norm_blocks)(functools.partial(run, True))
        pl.when(j >= n_norm_blocks)(functools.partial(run, False))


def norm_matmul(x, gains, gain_idx, w_stack, layer, head_gains, head_gain_idx, *, n, tm, tn, name,
                col0=0, dil=1):
    pair = isinstance(x, tuple)
    m = sum(part.shape[0] for part in x) if pair else x.shape[0]
    k = w_stack.shape[1]
    n_norm = head_gains.shape[2]
    n_blocks = n // tn
    n_norm_blocks = n_norm // tn
    assert m % tm == 0 and n % tn == 0 and n_norm % tn == 0 and n_norm_blocks >= 1 and col0 % tn == 0
    assert tn % MATMUL_CHUNK == 0 and tm % (16 * dil) == 0
    if pair:
        prompt_tiles, x_specs = _group_row_specs(tm, k)
        x_args = list(x)
    else:
        prompt_tiles, x_specs, x_args = None, [pl.BlockSpec((tm, k), lambda i, j: (i, 0))], [x]
    body = functools.partial(_norm_matmul_body, n_norm_blocks=n_norm_blocks, n_blocks=n_blocks,
                             tm=tm, tn=tn, dil=dil, prompt_tiles=prompt_tiles)
    scratch = [pltpu.VMEM((tm, k), BF16)]
    if dil > 1:
        assert dil % DEINTERLEAVE_STRIDE == 0 and dil // DEINTERLEAVE_STRIDE <= DEINTERLEAVE_STRIDE
        scratch.append(pltpu.VMEM((SLAB_RING, tm, HEAD_DIM), F32))
    if dil > DEINTERLEAVE_STRIDE:
        scratch.append(pltpu.VMEM((SLAB_RING, DEINTERLEAVE_STRIDE, tm // DEINTERLEAVE_STRIDE,
                                   HEAD_DIM), F32))
    return pl.pallas_call(
        body,
        grid=(m // tm, n_blocks),
        in_specs=x_specs + [
            _layer_spec((1, k), gain_idx, lambda i, j: (0, 0)),
            _layer_spec((k, tn), layer, lambda i, j: (0, col0 // tn + j)),
            _layer_spec((1, tn), head_gain_idx, lambda i, j: (0, jnp.minimum(j, n_norm_blocks - 1))),
        ],
        out_specs=pl.BlockSpec((dil, tm // dil, tn), lambda i, j: (0, i, j)),
        out_shape=jax.ShapeDtypeStruct((dil, m // dil, n), BF16),
        scratch_shapes=scratch,
        compiler_params=_params("arbitrary" if pair else "parallel", "arbitrary"),
        name=name,
    )(*x_args, gains, w_stack, head_gains)


def _project_cross_body(a_ref, w_ref, *refs, prompt_tiles):
    o_ref = refs[-1]
    if prompt_tiles is None:
        res, cross_refs = refs[0][...], refs[1:-1]
    else:
        res = jnp.where(pl.program_id(0) < prompt_tiles, refs[0][...], refs[1][...])
        cross_refs = refs[2:-1]
    x = res + jnp.dot(a_ref[...], w_ref[...], preferred_element_type=F32)
    o_ref[...] = _cross_residual(x, *cross_refs)


def project_cross_residual(a, w_stack, layer, res, cross_args, cross_layer, *, tm, name):
    m, k = a.shape
    n = w_stack.shape[2]
    assert m % tm == 0
    if isinstance(res, tuple):
        prompt_tiles, res_specs = _group_row_specs(tm, n)
        res_args = list(res)
    else:
        prompt_tiles, res_specs, res_args = None, [pl.BlockSpec((tm, n), lambda i: (i, 0))], [res]
    return pl.pallas_call(
        functools.partial(_project_cross_body, prompt_tiles=prompt_tiles),
        grid=(m // tm,),
        in_specs=[pl.BlockSpec((tm, k), lambda i: (i, 0)),
                  _layer_spec((k, n), layer, lambda i: (0, 0), resident=True)] + res_specs
        + _cross_specs(tm, cross_layer),
        out_specs=pl.BlockSpec((tm, n), lambda i: (i, 0)),
        out_shape=jax.ShapeDtypeStruct((m, n), F32),
        compiler_params=_params("arbitrary" if prompt_tiles is not None else "parallel"),
        name=name,
    )(a, w_stack, *res_args, *cross_args)


def _mlp_body(x_ref, g_ref, wu_ref, wd_ref, *refs, prompt_tiles):
    xn_ref = refs[-1]

    def step(o_ref):
        @pl.when(pl.program_id(1) == 0)
        def _():
            x = x_ref[...]
            xn_ref[...] = _norm_rows(x, g_ref[...]).astype(BF16)
            o_ref[...] = x

        h = jnp.dot(xn_ref[...], wu_ref[...], preferred_element_type=F32)
        h = jnp.maximum(h, 0.0)
        o_ref[...] += jnp.dot((h * h).astype(BF16), wd_ref[...], preferred_element_type=F32)

    if prompt_tiles is None:
        step(refs[0])
    else:
        i = pl.program_id(0)
        pl.when(i < prompt_tiles)(functools.partial(step, refs[0]))
        pl.when(i >= prompt_tiles)(functools.partial(step, refs[1]))


def mlp_residual(x, gains, w_up, w_down, layer, *, tm, tf, name, split_groups=False):
    m, d = x.shape
    ff = w_up.shape[2]
    assert m % tm == 0 and ff % tf == 0 and SEQ % tm == 0
    if split_groups:
        tp = SEQ // tm
        out_specs = [pl.BlockSpec((tm, d), lambda i, f: (jnp.minimum(i, tp - 1), 0)),
                     pl.BlockSpec((tm, d), lambda i, f: (jnp.maximum(i - tp, 0), 0))]
        out_shape = [jax.ShapeDtypeStruct((SEQ, d), F32), jax.ShapeDtypeStruct((m - SEQ, d), F32)]
    else:
        tp = None
        out_specs = pl.BlockSpec((tm, d), lambda i, f: (i, 0))
        out_shape = jax.ShapeDtypeStruct((m, d), F32)
    return pl.pallas_call(
        functools.partial(_mlp_body, prompt_tiles=tp),
        grid=(m // tm, ff // tf),
        in_specs=[
            pl.BlockSpec((tm, d), lambda i, f: (i, 0)),
            _layer_spec((1, d), layer, lambda i, f: (0, 0)),
            _layer_spec((d, tf), layer, lambda i, f: (0, f)),
            _layer_spec((tf, d), layer, lambda i, f: (f, 0)),
        ],
        out_specs=out_specs,
        out_shape=out_shape,
        scratch_shapes=[pltpu.VMEM((tm, d), BF16)],
        compiler_params=_params("arbitrary" if split_groups else "parallel", "arbitrary"),
        name=name,
    )(x, gains, w_up, w_down)


def _cross_residual(x, g_ref, wq_ref, qn_ref, k_ref, v_ref, wo_ref):
    xn = _norm_rows(x, g_ref[...]).astype(BF16)
    q = jnp.dot(xn, wq_ref[...], preferred_element_type=F32)
    heads = []
    for h in range(X_HEADS):
        sl = _head_slice(h)
        qh = _norm_rows(q[:, sl], qn_ref[...]).astype(BF16)
        s = lax.dot_general(qh, k_ref[:, sl], NT_DIMS, preferred_element_type=F32)
        oh, _, _ = _softmax_pv([s], [v_ref[:, sl]])
        heads.append(oh.astype(BF16))
    o = jnp.concatenate(heads, axis=-1)
    return x + jnp.dot(o, wo_ref[...], preferred_element_type=F32)


def _cross_specs(tm, layer, tile_of=lambda i: i):
    d = D_MODEL
    dx = X_HEADS * HEAD_DIM
    assert SEQ % tm == 0 and DEC_SEQ % tm == 0
    tiles_p = SEQ // tm
    tiles_s = DEC_SEQ // tm

    def mem_seq(i):
        return jnp.where(i < tiles_p, 0, 1 + (i - tiles_p) // tiles_s)

    return [
        _layer_spec((1, d), layer, lambda i: (0, 0)),
        _layer_spec((d, dx), layer, lambda i: (0, 0), resident=True),
        _layer_spec((1, HEAD_DIM), layer, lambda i: (0, 0)),
        pl.BlockSpec((N_MEM, dx), lambda i: (mem_seq(tile_of(i)), 0)),
        pl.BlockSpec((N_MEM, dx), lambda i: (mem_seq(tile_of(i)), 1)),
        _layer_spec((dx, d), layer, lambda i: (0, 0), resident=True),
    ]


def _na_block_info(i):
    in_prompt = i < NA_NBP
    local = jnp.where(in_prompt, i, (i - NA_NBP) % NA_NBS)
    nblk = jnp.where(in_prompt, NA_NBP, NA_NBS)
    base = jnp.where(in_prompt, 0, NA_NBP + ((i - NA_NBP) // NA_NBS) * NA_NBS)
    kblk = base + jnp.clip(local - 1, 0, nblk - NA_KBLKS)
    placement = jnp.where(local == 0, 0, jnp.where(local == nblk - 1, 2, 1))
    return kblk, placement


def _na_body(q_ref, k0_ref, k1_ref, k2_ref, v0_ref, v1_ref, v2_ref, b_ref, o_ref):
    k_refs = (k0_ref, k1_ref, k2_ref)
    v_refs = (v0_ref, v1_ref, v2_ref)
    ones = jnp.ones((NA_BLK, HEAD_DIM), BF16)
    for h in range(N_HEADS):
        sl = _head_slice(h)
        q = q_ref[:, sl]
        logits = []
        for c in range(NA_KBLKS):
            s = lax.dot_general(q, k_refs[c][:, sl], NT_DIMS, preferred_element_type=F32)
            logits.append(s + b_ref[0, h, :, c * NA_BLK:(c + 1) * NA_BLK])
        o, _, _ = _softmax_pv(logits, [v_ref[:, sl] for v_ref in v_refs], ones)
        o_ref[:, sl] = o.astype(o_ref.dtype)


def neighborhood_attention(qkv, bias_tables, layer, *, name):
    d = D_MODEL

    def kv_spec(c, part):
        return pl.BlockSpec((NA_BLK, d), lambda i: (_na_block_info(i)[0] + c, part))

    return pl.pallas_call(
        _na_body,
        grid=(N_TOK // NA_BLK,),
        in_specs=[pl.BlockSpec((NA_BLK, d), lambda i: (i, 0))]
        + [kv_spec(c, 1) for c in range(NA_KBLKS)]
        + [kv_spec(c, 2) for c in range(NA_KBLKS)]
        + [pl.BlockSpec((None, 1, N_HEADS, NA_BLK, NA_KBLKS * NA_BLK),
                        lambda i: (layer, _na_block_info(i)[1], 0, 0, 0))],
        out_specs=pl.BlockSpec((NA_BLK, d), lambda i: (i, 0)),
        out_shape=jax.ShapeDtypeStruct((N_TOK, d), BF16),
        compiler_params=_params("parallel"),
        name=name,
    )(*([qkv] * (1 + 2 * NA_KBLKS)), bias_tables)


def _toeplitz(v, n_rows, n_cols):
    period = v.shape[-1]
    assert period >= n_rows + n_cols - 1
    flat = jnp.tile(v, (1,) * (v.ndim - 1) + (n_rows,))[..., :n_rows * (period - 1)]
    return flat.reshape(v.shape[:-1] + (n_rows, period - 1))[..., :n_cols]


def na_bias_tables(rpb):
    n_layers = rpb.shape[0]
    n_krows = NA_KBLKS * NA_QROWS
    n_dr = 2 * NA_WIN_ROWS - 1
    assert n_krows - NA_WIN_ROWS == NA_QROWS
    c = np.arange(GRID_W)
    cs = np.clip(c - NA_WIN_COLS // 2, 0, GRID_W - NA_WIN_COLS)
    col_ok = (c[None, :] >= cs[:, None]) & (c[None, :] < cs[:, None] + NA_WIN_COLS)
    dist = np.arange(2 * GRID_W)
    dist = np.where(dist < GRID_W, dist, dist - 2 * GRID_W)
    by_dist = rpb.astype(F32)[..., np.clip(dist, -(NA_WIN_COLS - 1), NA_WIN_COLS - 1) + (NA_WIN_COLS - 1)]
    tiles = jnp.where(col_ok, _toeplitz(by_dist, GRID_W, GRID_W), NEG_INF)
    side_by_side = tiles.transpose(0, 1, 3, 2, 4).reshape(n_layers, N_HEADS, GRID_W, n_dr * GRID_W)

    def strip(first_dr, lead):
        window = side_by_side[..., first_dr * GRID_W:(first_dr + NA_WIN_ROWS) * GRID_W]
        pad = [(0, 0)] * 3 + [(lead * GRID_W, (n_krows - NA_WIN_ROWS - lead) * GRID_W)]
        return jnp.pad(window, pad, constant_values=NEG_INF)

    placements = (
        [strip(7 - rq, 0) for rq in range(NA_QROWS)],
        [strip(3, rq) for rq in range(NA_QROWS)],
        [strip(3 - rq, 4) for rq in range(NA_QROWS)],
    )
    return jnp.stack([jnp.concatenate(strips, axis=2) for strips in placements], axis=1)


def _dil_body(q_ref, k_ref, v_ref, kp_ref, kn_ref, vp_ref, vn_ref, b_ref, o_ref, lse_ref,
              kext_ref, vext_ref, *, dil, tl, planes):
    lse_ref[...] = jnp.zeros_like(lse_ref)

    row0 = pl.program_id(1) * tl
    rows_prompt = SEQ // dil
    len_sample = DEC_SEQ // dil
    in_prompt = row0 < rows_prompt
    pos0 = jnp.where(in_prompt, row0, (row0 - rows_prompt) % len_sample)
    seq_len = jnp.where(in_prompt, rows_prompt, len_sample)

    for p in range(planes):
        for ext_ref, prev_ref, cur_ref, next_ref in ((kext_ref, kp_ref, k_ref, kn_ref),
                                                     (vext_ref, vp_ref, v_ref, vn_ref)):
            ext_ref[p, 0:DIL_RADIUS] = prev_ref[p]
            ext_ref[p, DIL_RADIUS:DIL_RADIUS + tl] = cur_ref[p]
            ext_ref[p, DIL_RADIUS + tl:] = next_ref[p]

    blocks_per_plane = tl // DIL_BQ

    def block(item):
        p = item // blocks_per_plane
        qs = pl.multiple_of((item % blocks_per_plane) * DIL_BQ, DIL_BQ)
        rows = pl.ds(qs, DIL_BQ)
        win = pl.ds(qs, DIL_KW)
        kpos = pos0 + qs - DIL_RADIUS + lax.broadcasted_iota(jnp.int32, (1, DIL_KW), 1)
        edge = jnp.where((kpos >= 0) & (kpos < seq_len), 0.0, NEG_INF).astype(F32)
        for h in range(N_HEADS):
            sl = _head_slice(h)
            s = lax.dot_general(q_ref[p, rows, sl], kext_ref[p, win, sl], NT_DIMS,
                                preferred_element_type=F32)
            s = s + b_ref[0, h] + edge
            o, m, den = _softmax_pv([s], [vext_ref[p, win, sl]])
            o_ref[p, rows, sl] = o
            lse_ref[p, rows, h:h + 1] = m + jnp.log(den)

    def pair(k, carry):
        for u in range(DIL_UNROLL):
            block(k * DIL_UNROLL + u)
        return carry

    lax.fori_loop(0, planes * blocks_per_plane // DIL_UNROLL, pair, 0)


def dilated_group_attention(qkv, bias_table, *, group, name):
    d = D_MODEL
    dil = DIL_CONFIGS[group][1]
    assert DIL_CONFIGS[group][0] // (2 * dil) == DIL_RADIUS
    rows = N_TOK // dil
    tl = min(DIL_STEP_ROWS, DEC_SEQ // dil)
    planes = min(dil, DIL_STEP_ROWS // tl)
    assert (SEQ // dil) % tl == 0 and (DEC_SEQ // dil) % tl == 0 and tl % DIL_BQ == 0
    assert dil % planes == 0
    halo_per_tile = tl // DIL_RADIUS
    n_halo = rows // DIL_RADIUS

    def main_spec(part):
        return pl.BlockSpec((planes, tl, d), lambda r, t: (r, t, part))

    def prev_spec(part):
        return pl.BlockSpec((planes, DIL_RADIUS, d),
                            lambda r, t: (r, jnp.maximum(t * halo_per_tile - 1, 0), part))

    def next_spec(part):
        return pl.BlockSpec((planes, DIL_RADIUS, d),
                            lambda r, t: (r, jnp.minimum((t + 1) * halo_per_tile, n_halo - 1), part))

    return pl.pallas_call(
        functools.partial(_dil_body, dil=dil, tl=tl, planes=planes),
        grid=(dil // planes, rows // tl),
        in_specs=[main_spec(0), main_spec(1), main_spec(2),
                  prev_spec(1), next_spec(1), prev_spec(2), next_spec(2),
                  pl.BlockSpec((1, N_HEADS, DIL_BQ, DIL_KW), lambda r, t: (group, 0, 0, 0))],
        out_specs=[pl.BlockSpec((planes, tl, d), lambda r, t: (r, t, 0)),
                   pl.BlockSpec((planes, tl, HEAD_DIM), lambda r, t: (r, t, 0))],
        out_shape=[jax.ShapeDtypeStruct((dil, rows, d), F32),
                   jax.ShapeDtypeStruct((dil, rows, HEAD_DIM), F32)],
        scratch_shapes=[pltpu.VMEM((planes, tl + 2 * DIL_RADIUS, d), BF16),
                        pltpu.VMEM((planes, tl + 2 * DIL_RADIUS, d), BF16)],
        compiler_params=_params("parallel", "parallel"),
        name=name,
    )(*([qkv] * 7), bias_table)


def _t5_bucket(rel):
    nb = T5_BUCKETS // 2
    max_exact = nb // 2
    ret = jnp.where(rel > 0, nb, 0)
    n = jnp.abs(rel)
    n_f = jnp.maximum(n, 1).astype(F32)
    large = max_exact + (jnp.log(n_f / max_exact) / math.log(T5_MAX_DIST / max_exact)
                         * (nb - max_exact)).astype(jnp.int32)
    large = jnp.minimum(large, nb - 1)
    return ret + jnp.where(n < max_exact, n, large)


def dilated_bias_table(t5_table):
    period = DIL_BQ + DIL_KW
    rel = np.arange(-DIL_RADIUS, DIL_RADIUS + 1)
    pad = jnp.full((N_HEADS, period - rel.size), NEG_INF, F32)
    tables = []
    for g, (_, dil) in enumerate(DIL_CONFIGS):
        band = t5_table[:, g].astype(F32)[_t5_bucket(jnp.asarray(rel * dil, jnp.int32))]
        tables.append(_toeplitz(jnp.concatenate([band.T, pad], axis=-1), DIL_BQ, DIL_KW))
    return jnp.stack(tables)


def _merge_project_body(*refs, tm):
    o_refs = refs[:N_GROUPS]
    l_refs = refs[N_GROUPS:2 * N_GROUPS]
    w_ref, r_ref = refs[2 * N_GROUPS:2 * N_GROUPS + 2]
    cross_refs = refs[2 * N_GROUPS + 2:-5]
    out_ref, ltok_ref, otok_ref = refs[-5:-2]
    merged_refs = refs[-2:]
    dils = [dil for _, dil in DIL_CONFIGS]
    assert dils[0] == 1
    i = pl.program_id(0)

    @pl.when(i == 0)
    def _():
        merged_refs[1][...] = jnp.zeros_like(merged_refs[1])

    def to_token_order(dst_ref, idx, src, dil, r):
        if dil == 1:
            dst_ref[idx] = src
        else:
            dst_ref[idx + (pl.ds(r, tm // dil, stride=dil), slice(None))] = src

    def step(prev_ref, cur_ref):
        x = r_ref[...] + jnp.dot(prev_ref[...], w_ref[...], preferred_element_type=F32)
        out_ref[...] = _cross_residual(x, *cross_refs)

        for g, dil in enumerate(dils):
            for r in range(dil):
                to_token_order(ltok_ref, (g,), l_refs[g][r], dil, r)
        lses = [ltok_ref[g] for g in range(N_GROUPS)]
        top = functools.reduce(jnp.maximum, lses)
        wts = [jnp.exp(l - top) for l in lses]
        z = functools.reduce(jnp.add, wts)
        wts = [w / z for w in wts]
        for g, dil in enumerate(dils):
            if dil == 1:
                continue
            for r in range(dil):
                for h in range(N_HEADS):
                    to_token_order(otok_ref, (g - 1, h), o_refs[g][r, :, _head_slice(h)], dil, r)
        for h in range(N_HEADS):
            sl = _head_slice(h)
            parts = [o_refs[0][0, :, sl]] + [otok_ref[g - 1, h] for g in range(1, N_GROUPS)]
            mix = functools.reduce(jnp.add, [p * wts[g][:, h:h + 1] for g, p in enumerate(parts)])
            cur_ref[:, sl] = mix.astype(BF16)

    pl.when(i % 2 == 0)(functools.partial(step, merged_refs[1], merged_refs[0]))
    pl.when(i % 2 == 1)(functools.partial(step, merged_refs[0], merged_refs[1]))


def merge_project_cross_residual(outs, lses, w_stack, layer, res, cross_args, cross_layer, *, tm, name):
    m, d = res.shape
    dils = [dil for _, dil in DIL_CONFIGS]
    assert m % tm == 0 and all(tm % (8 * dil) == 0 for dil in dils)
    n_tiles = m // tm

    def merging(i):
        return jnp.minimum(i, n_tiles - 1)

    def projecting(i):
        return jnp.maximum(i - 1, 0)

    in_specs = [pl.BlockSpec((dil, tm // dil, d), lambda i: (0, merging(i), 0)) for dil in dils]
    in_specs += [pl.BlockSpec((dil, tm // dil, HEAD_DIM), lambda i: (0, merging(i), 0)) for dil in dils]
    in_specs += [_layer_spec((d, d), layer, lambda i: (0, 0), resident=True),
                 pl.BlockSpec((tm, d), lambda i: (projecting(i), 0))]
    in_specs += _cross_specs(tm, cross_layer, tile_of=projecting)
    return pl.pallas_call(
        functools.partial(_merge_project_body, tm=tm),
        grid=(n_tiles + 1,),
        in_specs=in_specs,
        out_specs=pl.BlockSpec((tm, d), lambda i: (projecting(i), 0)),
        out_shape=jax.ShapeDtypeStruct((m, d), F32),
        scratch_shapes=[pltpu.VMEM((N_GROUPS, tm, HEAD_DIM), F32),
                        pltpu.VMEM((N_GROUPS - 1, N_HEADS, tm, HEAD_DIM), F32),
                        pltpu.VMEM((tm, d), BF16),
                        pltpu.VMEM((tm, d), BF16)],
        compiler_params=_params("arbitrary"),
        name=name,
    )(*outs, *lses, w_stack, res, *cross_args)


def _qk_gains(q_gains, k_gains):
    tiled = [jnp.tile(g.astype(F32).reshape(-1, HEAD_DIM), (1, N_HEADS)) for g in (q_gains, k_gains)]
    return jnp.concatenate([tiled[0] * ATTN_SCALE, tiled[1]], axis=1)[:, None]


def kernel(x_prompt, x_sample, mem_prompt, mem_sample, g_mix, g_cross, g_mem, g_mlp, w_qkv_a, q_norm_a, k_norm_a, rpb_a, w_o_a, w_qkv_b, q_norm_b, k_norm_b, t5_table, w_o_b, w_q_x, w_kv_x, q_norm_x, k_norm_x, w_o_x, w_up, w_down):
    d = D_MODEL
    x = (x_prompt.reshape(SEQ, d), x_sample.reshape(DEC_BATCH * DEC_SEQ, d))
    mem = jnp.concatenate([mem_prompt.reshape(N_MEM, d), mem_sample.reshape(DEC_BATCH * N_MEM, d)])
    dil_bias = dilated_bias_table(t5_table)
    na_bias = na_bias_tables(rpb_a)
    (w_qkv_a, w_o_a, w_qkv_b, w_o_b, w_q_x, w_kv_x, w_o_x, w_up, w_down) = (
        w.astype(BF16) for w in (w_qkv_a, w_o_a, w_qkv_b, w_o_b, w_q_x, w_kv_x, w_o_x, w_up, w_down))
    g_mix, g_cross, g_mem, g_mlp = (g.astype(F32)[:, None] for g in (g_mix, g_cross, g_mem, g_mlp))
    qk_gain_a = _qk_gains(q_norm_a, k_norm_a)
    qk_gain_b = _qk_gains(q_norm_b, k_norm_b)
    k_gain_x = jnp.tile(k_norm_x.astype(F32), (1, X_HEADS))[:, None]
    q_gain_x = (q_norm_x.astype(F32) * ATTN_SCALE)[:, None]

    def cross_args(i):
        kv = norm_matmul(mem, g_mem, i, w_kv_x, i, k_gain_x, i,
                         n=2 * X_HEADS * HEAD_DIM, tm=N_SEQS * N_MEM, tn=X_HEADS * HEAD_DIM,
                         name=f"kv_x{i}").reshape(N_SEQS * N_MEM, 2 * X_HEADS * HEAD_DIM)
        return (g_cross, w_q_x, q_gain_x, kv, kv, w_o_x)

    for i in range(DEPTH):
        li = i // 2
        if i % 2 == 0:
            qkv = norm_matmul(x, g_mix, i, w_qkv_a, li, qk_gain_a, li,
                              n=3 * d, tm=1024, tn=1024 if isinstance(x, tuple) else 2048,
                              name=f"qkv_a{i}")
            o = neighborhood_attention(qkv.reshape(N_TOK, 3 * d), na_bias, li, name=f"na_attn{i}")
            x = project_cross_residual(o, w_o_a, li, x, cross_args(i), i, tm=512, name=f"wo_cross{i}")
        else:
            outs, lses = [], []
            for g, (_, dil) in enumerate(DIL_CONFIGS):
                qkv = norm_matmul(x, g_mix, i, w_qkv_b, li, qk_gain_b, li * N_GROUPS + g,
                                  n=3 * d, col0=g * 3 * d, tm=1024, tn=2048, dil=dil,
                                  name=f"qkv_b{i}_{g}")
                o_g, lse_g = dilated_group_attention(qkv, dil_bias, group=g, name=f"dil_attn{i}_{g}")
                outs.append(o_g)
                lses.append(lse_g)
            x = merge_project_cross_residual(outs, lses, w_o_b, li, x, cross_args(i), i,
                                             tm=256, name=f"wo_cross{i}")
        x = mlp_residual(x, g_mlp, w_up, w_down, i,
                         tm=512, tf=1024, split_groups=(i == DEPTH - 1), name=f"mlp{i}")

    y_prompt, y_sample = x
    return (y_prompt.reshape(1, SEQ, d), y_sample.reshape(DEC_BATCH, DEC_SEQ, d))
```

```python
import functools
import math

import numpy as np
import jax
import jax.numpy as jnp
from jax import lax
from jax.experimental import pallas as pl
from jax.experimental.pallas import tpu as pltpu

D_MODEL = 2048
SEQ = 8192
DEPTH = 4
DEC_BATCH = 4
DEC_SEQ = 2048
N_TOK = SEQ + DEC_BATCH * DEC_SEQ
N_SEQS = 1 + DEC_BATCH

HEAD_DIM = 128
N_HEADS = D_MODEL // HEAD_DIM
NA_WIN_ROWS = 8
NA_WIN_COLS = 16
GRID_W = 64
DIL_CONFIGS = ((128, 1), (512, 4), (2048, 16))
N_GROUPS = len(DIL_CONFIGS)
DIL_RADIUS = 64
T5_BUCKETS = 32
T5_MAX_DIST = 1024
X_HEADS = 4
N_MEM = 256
D_FF = 4 * D_MODEL
RMS_EPS = 1e-6
ATTN_SCALE = 1.0 / math.sqrt(HEAD_DIM)
NEG_INF = float("-inf")

F32 = jnp.float32
BF16 = jnp.bfloat16
NT_DIMS = (((1,), (1,)), ((), ()))

VMEM_LIMIT_BYTES = 56 * 1024 * 1024

NA_QROWS = 4
NA_BLK = NA_QROWS * GRID_W
NA_KBLKS = 3
NA_NBP = SEQ // NA_BLK
NA_NBS = DEC_SEQ // NA_BLK

DIL_BQ = 128
DIL_KW = DIL_BQ + 2 * DIL_RADIUS
DIL_STEP_ROWS = 512

QKV_ROWS, QKV_COLS = 1024, 2048
QKV_COLS_PAIRED = 1024
PROJ_ROWS = 512
MERGE_ROWS = 256
MLP_ROWS, MLP_CHUNK = 512, 1024

MATMUL_CHUNK = 256
DEINTERLEAVE_STRIDE = 4
SLAB_RING = 4


def _params(*sem):
    return pltpu.CompilerParams(dimension_semantics=sem, vmem_limit_bytes=VMEM_LIMIT_BYTES)


def _layer_spec(block, layer, index_map, resident=False):
    return pl.BlockSpec((None,) + block, lambda *ids: (layer,) + index_map(*ids),
                        pipeline_mode=pl.Buffered(1) if resident else None)


def _group_row_specs(tm, width):
    tp = SEQ // tm
    return tp, [pl.BlockSpec((tm, width), lambda i, *_: (jnp.minimum(i, tp - 1), 0)),
                pl.BlockSpec((tm, width), lambda i, *_: (jnp.maximum(i - tp, 0), 0))]


def _norm_rows(x, g):
    ms = jnp.mean(x * x, axis=-1, keepdims=True)
    return x * lax.rsqrt(ms + RMS_EPS) * g


def _head_slice(h):
    return slice(h * HEAD_DIM, (h + 1) * HEAD_DIM)


def _softmax_pv(logits, values, ones=None):
    m = functools.reduce(jnp.maximum, [jnp.max(s, axis=-1, keepdims=True) for s in logits])
    if ones is None:
        probs = [jnp.exp(s - m) for s in logits]
        den = functools.reduce(jnp.add, [jnp.sum(p, axis=-1, keepdims=True) for p in probs])
        acc = functools.reduce(jnp.add, [jnp.dot(p.astype(BF16), v, preferred_element_type=F32)
                                         for p, v in zip(probs, values)])
        return acc / den, m, den
    res = functools.reduce(jnp.add, [
        jnp.dot(jnp.exp((s - m).astype(BF16)), jnp.concatenate([v, ones], axis=1),
                preferred_element_type=F32)
        for s, v in zip(logits, values)])
    den = res[:, HEAD_DIM:]
    return res[:, :HEAD_DIM] / den, m, den[:, :1]


def _norm_matmul_body(*refs, n_norm_blocks, n_blocks, tm, tn, dil, prompt_tiles):
    n_x = 1 if prompt_tiles is None else 2
    x_refs = refs[:n_x]
    g_ref, w_ref, hg_ref, o_ref, xn_ref = refs[n_x:n_x + 5]
    slab_refs = refs[n_x + 5:]
    i = pl.program_id(0)
    j = pl.program_id(1)

    def normalise(x_ref):
        xn_ref[...] = _norm_rows(x_ref[...], g_ref[...]).astype(BF16)

    if prompt_tiles is None:
        pl.when(j == 0)(functools.partial(normalise, x_refs[0]))
    else:
        pl.when((j == 0) & (i < prompt_tiles))(functools.partial(normalise, x_refs[0]))
        pl.when((j == 0) & (i >= prompt_tiles))(functools.partial(normalise, x_refs[1]))

    def run(head_norm):
        for c in range(tn // MATMUL_CHUNK):
            acc = jnp.dot(xn_ref[...], w_ref[:, c * MATMUL_CHUNK:(c + 1) * MATMUL_CHUNK],
                          preferred_element_type=F32)
            for s in range(MATMUL_CHUNK // HEAD_DIM):
                slab = c * (MATMUL_CHUNK // HEAD_DIM) + s
                sl = _head_slice(slab)
                piece = acc[:, _head_slice(s)]
                if head_norm:
                    piece = _norm_rows(piece, hg_ref[:, sl])
                if dil == 1:
                    o_ref[0, :, sl] = piece.astype(o_ref.dtype)
                    continue
                slab_ref = slab_refs[0]
                slot = slab % SLAB_RING
                slab_ref[slot] = piece
                if dil == DEINTERLEAVE_STRIDE:
                    for r in range(dil):
                        rows = slab_ref[slot, pl.ds(r, tm // dil, stride=dil), :]
                        o_ref[r, :, sl] = rows.astype(o_ref.dtype)
                else:
                    quarter_ref = slab_refs[1]
                    outer = dil // DEINTERLEAVE_STRIDE
                    for b in range(DEINTERLEAVE_STRIDE):
                        quarter_ref[slot, b] = slab_ref[
                            slot, pl.ds(b, tm // DEINTERLEAVE_STRIDE, stride=DEINTERLEAVE_STRIDE), :]
                        for a in range(outer):
                            rows = quarter_ref[slot, b, pl.ds(a, tm // dil, stride=outer), :]
                            o_ref[a * DEINTERLEAVE_STRIDE + b, :, sl] = rows.astype(o_ref.dtype)

    if n_norm_blocks == n_blocks:
        run(True)
    else:
        pl.when(j < n_norm_blocks)(functools.partial(run, True))
        pl.when(j >= n_norm_blocks)(functools.partial(run, False))


def norm_matmul(x, gains, gain_idx, w_stack, layer, head_gains, head_gain_idx, *, n, tm, tn, name,
                col0=0, dil=1):
    pair = isinstance(x, tuple)
    m = sum(part.shape[0] for part in x) if pair else x.shape[0]
    k = w_stack.shape[1]
    n_norm = head_gains.shape[2]
    n_blocks = n // tn
    n_norm_blocks = n_norm // tn
    assert m % tm == 0 and n % tn == 0 and n_norm % tn == 0 and n_norm_blocks >= 1 and col0 % tn == 0
    assert tn % MATMUL_CHUNK == 0 and tm % (16 * dil) == 0
    if pair:
        prompt_tiles, x_specs = _group_row_specs(tm, k)
        x_args = list(x)
    else:
        prompt_tiles, x_specs, x_args = None, [pl.BlockSpec((tm, k), lambda i, j: (i, 0))], [x]
    body = functools.partial(_norm_matmul_body, n_norm_blocks=n_norm_blocks, n_blocks=n_blocks,
                             tm=tm, tn=tn, dil=dil, prompt_tiles=prompt_tiles)
    scratch = [pltpu.VMEM((tm, k), BF16)]
    if dil > 1:
        assert dil % DEINTERLEAVE_STRIDE == 0 and dil // DEINTERLEAVE_STRIDE <= DEINTERLEAVE_STRIDE
        scratch.append(pltpu.VMEM((SLAB_RING, tm, HEAD_DIM), F32))
    if dil > DEINTERLEAVE_STRIDE:
        scratch.append(pltpu.VMEM((SLAB_RING, DEINTERLEAVE_STRIDE, tm // DEINTERLEAVE_STRIDE,
                                   HEAD_DIM), F32))
    return pl.pallas_call(
        body,
        grid=(m // tm, n_blocks),
        in_specs=x_specs + [
            _layer_spec((1, k), gain_idx, lambda i, j: (0, 0)),
            _layer_spec((k, tn), layer, lambda i, j: (0, col0 // tn + j)),
            _layer_spec((1, tn), head_gain_idx, lambda i, j: (0, jnp.minimum(j, n_norm_blocks - 1))),
        ],
        out_specs=pl.BlockSpec((dil, tm // dil, tn), lambda i, j: (0, i, j)),
        out_shape=jax.ShapeDtypeStruct((dil, m // dil, n), BF16),
        scratch_shapes=scratch,
        compiler_params=_params("arbitrary" if pair else "parallel", "arbitrary"),
        name=name,
    )(*x_args, gains, w_stack, head_gains)


def _project_cross_body(a_ref, w_ref, *refs, prompt_tiles):
    o_ref = refs[-1]
    if prompt_tiles is None:
        res, cross_refs = refs[0][...], refs[1:-1]
    else:
        res = jnp.where(pl.program_id(0) < prompt_tiles, refs[0][...], refs[1][...])
        cross_refs = refs[2:-1]
    x = res + jnp.dot(a_ref[...], w_ref[...], preferred_element_type=F32)
    o_ref[...] = _cross_residual(x, *cross_refs)


def project_cross_residual(a, w_stack, layer, res, cross_args, cross_layer, *, tm, name):
    m, k = a.shape
    n = w_stack.shape[2]
    assert m % tm == 0
    if isinstance(res, tuple):
        prompt_tiles, res_specs = _group_row_specs(tm, n)
        res_args = list(res)
    else:
        prompt_tiles, res_specs, res_args = None, [pl.BlockSpec((tm, n), lambda i: (i, 0))], [res]
    return pl.pallas_call(
        functools.partial(_project_cross_body, prompt_tiles=prompt_tiles),
        grid=(m // tm,),
        in_specs=[pl.BlockSpec((tm, k), lambda i: (i, 0)),
                  _layer_spec((k, n), layer, lambda i: (0, 0), resident=True)] + res_specs
        + _cross_specs(tm, cross_layer),
        out_specs=pl.BlockSpec((tm, n), lambda i: (i, 0)),
        out_shape=jax.ShapeDtypeStruct((m, n), F32),
        compiler_params=_params("arbitrary" if prompt_tiles is not None else "parallel"),
        name=name,
    )(a, w_stack, *res_args, *cross_args)


def _mlp_body(x_ref, g_ref, wu_ref, wd_ref, *refs, prompt_tiles):
    xn_ref = refs[-1]

    def step(o_ref):
        @pl.when(pl.program_id(1) == 0)
        def _():
            x = x_ref[...]
            xn_ref[...] = _norm_rows(x, g_ref[...]).astype(BF16)
            o_ref[...] = x

        h = jnp.dot(xn_ref[...], wu_ref[...], preferred_element_type=F32)
        h = jnp.maximum(h, 0.0)
        o_ref[...] += jnp.dot((h * h).astype(BF16), wd_ref[...], preferred_element_type=F32)

    if prompt_tiles is None:
        step(refs[0])
    else:
        i = pl.program_id(0)
        pl.when(i < prompt_tiles)(functools.partial(step, refs[0]))
        pl.when(i >= prompt_tiles)(functools.partial(step, refs[1]))


def mlp_residual(x, gains, w_up, w_down, layer, *, tm, tf, name, split_groups=False):
    m, d = x.shape
    ff = w_up.shape[2]
    assert m % tm == 0 and ff % tf == 0 and SEQ % tm == 0
    if split_groups:
        tp = SEQ // tm
        out_specs = [pl.BlockSpec((tm, d), lambda i, f: (jnp.minimum(i, tp - 1), 0)),
                     pl.BlockSpec((tm, d), lambda i, f: (jnp.maximum(i - tp, 0), 0))]
        out_shape = [jax.ShapeDtypeStruct((SEQ, d), F32), jax.ShapeDtypeStruct((m - SEQ, d), F32)]
    else:
        tp = None
        out_specs = pl.BlockSpec((tm, d), lambda i, f: (i, 0))
        out_shape = jax.ShapeDtypeStruct((m, d), F32)
    return pl.pallas_call(
        functools.partial(_mlp_body, prompt_tiles=tp),
        grid=(m // tm, ff // tf),
        in_specs=[
            pl.BlockSpec((tm, d), lambda i, f: (i, 0)),
            _layer_spec((1, d), layer, lambda i, f: (0, 0)),
            _layer_spec((d, tf), layer, lambda i, f: (0, f)),
            _layer_spec((tf, d), layer, lambda i, f: (f, 0)),
        ],
        out_specs=out_specs,
        out_shape=out_shape,
        scratch_shapes=[pltpu.VMEM((tm, d), BF16)],
        compiler_params=_params("arbitrary" if split_groups else "parallel", "arbitrary"),
        name=name,
    )(x, gains, w_up, w_down)


def _cross_residual(x, g_ref, wq_ref, qn_ref, k_ref, v_ref, wo_ref):
    xn = _norm_rows(x, g_ref[...]).astype(BF16)
    q = jnp.dot(xn, wq_ref[...], preferred_element_type=F32)
    heads = []
    for h in range(X_HEADS):
        sl = _head_slice(h)
        qh = _norm_rows(q[:, sl], qn_ref[...]).astype(BF16)
        s = lax.dot_general(qh, k_ref[:, sl], NT_DIMS, preferred_element_type=F32)
        oh, _, _ = _softmax_pv([s], [v_ref[:, sl]])
        heads.append(oh.astype(BF16))
    o = jnp.concatenate(heads, axis=-1)
    return x + jnp.dot(o, wo_ref[...], preferred_element_type=F32)


def _cross_specs(tm, layer, tile_of=lambda i: i):
    d = D_MODEL
    dx = X_HEADS * HEAD_DIM
    assert SEQ % tm == 0 and DEC_SEQ % tm == 0
    tiles_p = SEQ // tm
    tiles_s = DEC_SEQ // tm

    def mem_seq(i):
        return jnp.where(i < tiles_p, 0, 1 + (i - tiles_p) // tiles_s)

    return [
        _layer_spec((1, d), layer, lambda i: (0, 0)),
        _layer_spec((d, dx), layer, lambda i: (0, 0), resident=True),
        _layer_spec((1, HEAD_DIM), layer, lambda i: (0, 0)),
        pl.BlockSpec((N_MEM, dx), lambda i: (mem_seq(tile_of(i)), 0)),
        pl.BlockSpec((N_MEM, dx), lambda i: (mem_seq(tile_of(i)), 1)),
        _layer_spec((dx, d), layer, lambda i: (0, 0), resident=True),
    ]


def _na_block_info(i):
    in_prompt = i < NA_NBP
    local = jnp.where(in_prompt, i, (i - NA_NBP) % NA_NBS)
    nblk = jnp.where(in_prompt, NA_NBP, NA_NBS)
    base = jnp.where(in_prompt, 0, NA_NBP + ((i - NA_NBP) // NA_NBS) * NA_NBS)
    kblk = base + jnp.clip(local - 1, 0, nblk - NA_KBLKS)
    placement = jnp.where(local == 0, 0, jnp.where(local == nblk - 1, 2, 1))
    return kblk, placement


def _na_body(q_ref, k0_ref, k1_ref, k2_ref, v0_ref, v1_ref, v2_ref, b_ref, o_ref):
    k_refs = (k0_ref, k1_ref, k2_ref)
    v_refs = (v0_ref, v1_ref, v2_ref)
    ones = jnp.ones((NA_BLK, HEAD_DIM), BF16)
    for h in range(N_HEADS):
        sl = _head_slice(h)
        q = q_ref[:, sl]
        logits = []
        for c in range(NA_KBLKS):
            s = lax.dot_general(q, k_refs[c][:, sl], NT_DIMS, preferred_element_type=F32)
            logits.append(s + b_ref[0, h, :, c * NA_BLK:(c + 1) * NA_BLK])
        o, _, _ = _softmax_pv(logits, [v_ref[:, sl] for v_ref in v_refs], ones)
        o_ref[:, sl] = o.astype(o_ref.dtype)


def neighborhood_attention(qkv, bias_tables, layer, *, name):
    d = D_MODEL

    def kv_spec(c, part):
        return pl.BlockSpec((NA_BLK, d), lambda i: (_na_block_info(i)[0] + c, part))

    return pl.pallas_call(
        _na_body,
        grid=(N_TOK // NA_BLK,),
        in_specs=[pl.BlockSpec((NA_BLK, d), lambda i: (i, 0))]
        + [kv_spec(c, 1) for c in range(NA_KBLKS)]
        + [kv_spec(c, 2) for c in range(NA_KBLKS)]
        + [pl.BlockSpec((None, 1, N_HEADS, NA_BLK, NA_KBLKS * NA_BLK),
                        lambda i: (layer, _na_block_info(i)[1], 0, 0, 0))],
        out_specs=pl.BlockSpec((NA_BLK, d), lambda i: (i, 0)),
        out_shape=jax.ShapeDtypeStruct((N_TOK, d), BF16),
        compiler_params=_params("parallel"),
        name=name,
    )(*([qkv] * (1 + 2 * NA_KBLKS)), bias_tables)


def _toeplitz(v, n_rows, n_cols):
    period = v.shape[-1]
    assert period >= n_rows + n_cols - 1
    flat = jnp.tile(v, (1,) * (v.ndim - 1) + (n_rows,))[..., :n_rows * (period - 1)]
    return flat.reshape(v.shape[:-1] + (n_rows, period - 1))[..., :n_cols]


def na_bias_tables(rpb):
    n_layers = rpb.shape[0]
    n_krows = NA_KBLKS * NA_QROWS
    n_dr = 2 * NA_WIN_ROWS - 1
    assert n_krows - NA_WIN_ROWS == NA_QROWS
    c = np.arange(GRID_W)
    cs = np.clip(c - NA_WIN_COLS // 2, 0, GRID_W - NA_WIN_COLS)
    col_ok = (c[None, :] >= cs[:, None]) & (c[None, :] < cs[:, None] + NA_WIN_COLS)
    dist = np.arange(2 * GRID_W)
    dist = np.where(dist < GRID_W, dist, dist - 2 * GRID_W)
    by_dist = rpb.astype(F32)[..., np.clip(dist, -(NA_WIN_COLS - 1), NA_WIN_COLS - 1) + (NA_WIN_COLS - 1)]
    tiles = jnp.where(col_ok, _toeplitz(by_dist, GRID_W, GRID_W), NEG_INF)
    side_by_side = tiles.transpose(0, 1, 3, 2, 4).reshape(n_layers, N_HEADS, GRID_W, n_dr * GRID_W)

    def strip(first_dr, lead):
        window = side_by_side[..., first_dr * GRID_W:(first_dr + NA_WIN_ROWS) * GRID_W]
        pad = [(0, 0)] * 3 + [(lead * GRID_W, (n_krows - NA_WIN_ROWS - lead) * GRID_W)]
        return jnp.pad(window, pad, constant_values=NEG_INF)

    placements = (
        [strip(7 - rq, 0) for rq in range(NA_QROWS)],
        [strip(3, rq) for rq in range(NA_QROWS)],
        [strip(3 - rq, 4) for rq in range(NA_QROWS)],
    )
    return jnp.stack([jnp.concatenate(strips, axis=2) for strips in placements], axis=1)


def _dil_body(q_ref, k_ref, v_ref, kp_ref, kn_ref, vp_ref, vn_ref, b_ref, o_ref, lse_ref,
              kext_ref, vext_ref, *, dil, tl, planes):
    lse_ref[...] = jnp.zeros_like(lse_ref)

    row0 = pl.program_id(1) * tl
    rows_prompt = SEQ // dil
    len_sample = DEC_SEQ // dil
    in_prompt = row0 < rows_prompt
    pos0 = jnp.where(in_prompt, row0, (row0 - rows_prompt) % len_sample)
    seq_len = jnp.where(in_prompt, rows_prompt, len_sample)

    for p in range(planes):
        for ext_ref, prev_ref, cur_ref, next_ref in ((kext_ref, kp_ref, k_ref, kn_ref),
                                                     (vext_ref, vp_ref, v_ref, vn_ref)):
            ext_ref[p, 0:DIL_RADIUS] = prev_ref[p]
            ext_ref[p, DIL_RADIUS:DIL_RADIUS + tl] = cur_ref[p]
            ext_ref[p, DIL_RADIUS + tl:] = next_ref[p]

        for qs in range(0, tl, DIL_BQ):
            rows = slice(qs, qs + DIL_BQ)
            win = slice(qs, qs + DIL_KW)
            kpos = pos0 + qs - DIL_RADIUS + lax.broadcasted_iota(jnp.int32, (1, DIL_KW), 1)
            edge = jnp.where((kpos >= 0) & (kpos < seq_len), 0.0, NEG_INF).astype(F32)
            for h in range(N_HEADS):
                sl = _head_slice(h)
                s = lax.dot_general(q_ref[p, rows, sl], kext_ref[p, win, sl], NT_DIMS,
                                    preferred_element_type=F32)
                s = s + b_ref[0, h] + edge
                o, m, den = _softmax_pv([s], [vext_ref[p, win, sl]])
                o_ref[p, rows, sl] = o
                lse_ref[p, rows, h:h + 1] = m + jnp.log(den)


def dilated_group_attention(qkv, bias_table, *, group, name):
    d = D_MODEL
    dil = DIL_CONFIGS[group][1]
    assert DIL_CONFIGS[group][0] // (2 * dil) == DIL_RADIUS
    rows = N_TOK // dil
    tl = min(DIL_STEP_ROWS, DEC_SEQ // dil)
    planes = min(dil, DIL_STEP_ROWS // tl)
    assert (SEQ // dil) % tl == 0 and (DEC_SEQ // dil) % tl == 0 and tl % DIL_BQ == 0
    assert dil % planes == 0
    halo_per_tile = tl // DIL_RADIUS
    n_halo = rows // DIL_RADIUS

    def main_spec(part):
        return pl.BlockSpec((planes, tl, d), lambda r, t: (r, t, part))

    def prev_spec(part):
        return pl.BlockSpec((planes, DIL_RADIUS, d),
                            lambda r, t: (r, jnp.maximum(t * halo_per_tile - 1, 0), part))

    def next_spec(part):
        return pl.BlockSpec((planes, DIL_RADIUS, d),
                            lambda r, t: (r, jnp.minimum((t + 1) * halo_per_tile, n_halo - 1), part))

    return pl.pallas_call(
        functools.partial(_dil_body, dil=dil, tl=tl, planes=planes),
        grid=(dil // planes, rows // tl),
        in_specs=[main_spec(0), main_spec(1), main_spec(2),
                  prev_spec(1), next_spec(1), prev_spec(2), next_spec(2),
                  pl.BlockSpec((1, N_HEADS, DIL_BQ, DIL_KW), lambda r, t: (group, 0, 0, 0))],
        out_specs=[pl.BlockSpec((planes, tl, d), lambda r, t: (r, t, 0)),
                   pl.BlockSpec((planes, tl, HEAD_DIM), lambda r, t: (r, t, 0))],
        out_shape=[jax.ShapeDtypeStruct((dil, rows, d), F32),
                   jax.ShapeDtypeStruct((dil, rows, HEAD_DIM), F32)],
        scratch_shapes=[pltpu.VMEM((planes, tl + 2 * DIL_RADIUS, d), BF16),
                        pltpu.VMEM((planes, tl + 2 * DIL_RADIUS, d), BF16)],
        compiler_params=_params("parallel", "parallel"),
        name=name,
    )(*([qkv] * 7), bias_table)


def _t5_bucket(rel):
    nb = T5_BUCKETS // 2
    max_exact = nb // 2
    ret = jnp.where(rel > 0, nb, 0)
    n = jnp.abs(rel)
    n_f = jnp.maximum(n, 1).astype(F32)
    large = max_exact + (jnp.log(n_f / max_exact) / math.log(T5_MAX_DIST / max_exact)
                         * (nb - max_exact)).astype(jnp.int32)
    large = jnp.minimum(large, nb - 1)
    return ret + jnp.where(n < max_exact, n, large)


def dilated_bias_table(t5_table):
    period = DIL_BQ + DIL_KW
    rel = np.arange(-DIL_RADIUS, DIL_RADIUS + 1)
    pad = jnp.full((N_HEADS, period - rel.size), NEG_INF, F32)
    tables = []
    for g, (_, dil) in enumerate(DIL_CONFIGS):
        band = t5_table[:, g].astype(F32)[_t5_bucket(jnp.asarray(rel * dil, jnp.int32))]
        tables.append(_toeplitz(jnp.concatenate([band.T, pad], axis=-1), DIL_BQ, DIL_KW))
    return jnp.stack(tables)


def _merge_project_body(*refs, tm):
    o_refs = refs[:N_GROUPS]
    l_refs = refs[N_GROUPS:2 * N_GROUPS]
    w_ref, r_ref = refs[2 * N_GROUPS:2 * N_GROUPS + 2]
    cross_refs = refs[2 * N_GROUPS + 2:-5]
    out_ref, ltok_ref, otok_ref = refs[-5:-2]
    merged_refs = refs[-2:]
    dils = [dil for _, dil in DIL_CONFIGS]
    assert dils[0] == 1
    i = pl.program_id(0)

    @pl.when(i == 0)
    def _():
        merged_refs[1][...] = jnp.zeros_like(merged_refs[1])

    def to_token_order(dst_ref, idx, src, dil, r):
        if dil == 1:
            dst_ref[idx] = src
        else:
            dst_ref[idx + (pl.ds(r, tm // dil, stride=dil), slice(None))] = src

    def step(prev_ref, cur_ref):
        x = r_ref[...] + jnp.dot(prev_ref[...], w_ref[...], preferred_element_type=F32)
        out_ref[...] = _cross_residual(x, *cross_refs)

        for g, dil in enumerate(dils):
            for r in range(dil):
                to_token_order(ltok_ref, (g,), l_refs[g][r], dil, r)
        lses = [ltok_ref[g] for g in range(N_GROUPS)]
        top = functools.reduce(jnp.maximum, lses)
        wts = [jnp.exp(l - top) for l in lses]
        z = functools.reduce(jnp.add, wts)
        wts = [w / z for w in wts]
        for g, dil in enumerate(dils):
            if dil == 1:
                continue
            for r in range(dil):
                for h in range(N_HEADS):
                    to_token_order(otok_ref, (g - 1, h), o_refs[g][r, :, _head_slice(h)], dil, r)
        for h in range(N_HEADS):
            sl = _head_slice(h)
            parts = [o_refs[0][0, :, sl]] + [otok_ref[g - 1, h] for g in range(1, N_GROUPS)]
            mix = functools.reduce(jnp.add, [p * wts[g][:, h:h + 1] for g, p in enumerate(parts)])
            cur_ref[:, sl] = mix.astype(BF16)

    pl.when(i % 2 == 0)(functools.partial(step, merged_refs[1], merged_refs[0]))
    pl.when(i % 2 == 1)(functools.partial(step, merged_refs[0], merged_refs[1]))


def merge_project_cross_residual(outs, lses, w_stack, layer, res, cross_args, cross_layer, *, tm, name):
    m, d = res.shape
    dils = [dil for _, dil in DIL_CONFIGS]
    assert m % tm == 0 and all(tm % (8 * dil) == 0 for dil in dils)
    n_tiles = m // tm

    def merging(i):
        return jnp.minimum(i, n_tiles - 1)

    def projecting(i):
        return jnp.maximum(i - 1, 0)

    in_specs = [pl.BlockSpec((dil, tm // dil, d), lambda i: (0, merging(i), 0)) for dil in dils]
    in_specs += [pl.BlockSpec((dil, tm // dil, HEAD_DIM), lambda i: (0, merging(i), 0)) for dil in dils]
    in_specs += [_layer_spec((d, d), layer, lambda i: (0, 0), resident=True),
                 pl.BlockSpec((tm, d), lambda i: (projecting(i), 0))]
    in_specs += _cross_specs(tm, cross_layer, tile_of=projecting)
    return pl.pallas_call(
        functools.partial(_merge_project_body, tm=tm),
        grid=(n_tiles + 1,),
        in_specs=in_specs,
        out_specs=pl.BlockSpec((tm, d), lambda i: (projecting(i), 0)),
        out_shape=jax.ShapeDtypeStruct((m, d), F32),
        scratch_shapes=[pltpu.VMEM((N_GROUPS, tm, HEAD_DIM), F32),
                        pltpu.VMEM((N_GROUPS - 1, N_HEADS, tm, HEAD_DIM), F32),
                        pltpu.VMEM((tm, d), BF16),
                        pltpu.VMEM((tm, d), BF16)],
        compiler_params=_params("arbitrary"),
        name=name,
    )(*outs, *lses, w_stack, res, *cross_args)


def _qk_gains(q_gains, k_gains):
    tiled = [jnp.tile(g.astype(F32).reshape(-1, HEAD_DIM), (1, N_HEADS)) for g in (q_gains, k_gains)]
    return jnp.concatenate([tiled[0] * ATTN_SCALE, tiled[1]], axis=1)[:, None]


def kernel(x_prompt, x_sample, mem_prompt, mem_sample, g_mix, g_cross, g_mem, g_mlp, w_qkv_a, q_norm_a, k_norm_a, rpb_a, w_o_a, w_qkv_b, q_norm_b, k_norm_b, t5_table, w_o_b, w_q_x, w_kv_x, q_norm_x, k_norm_x, w_o_x, w_up, w_down):
    d = D_MODEL
    x = (x_prompt.reshape(SEQ, d), x_sample.reshape(DEC_BATCH * DEC_SEQ, d))
    mem = jnp.concatenate([mem_prompt.reshape(N_MEM, d), mem_sample.reshape(DEC_BATCH * N_MEM, d)])
    dil_bias = dilated_bias_table(t5_table)
    na_bias = na_bias_tables(rpb_a)
    (w_qkv_a, w_o_a, w_qkv_b, w_o_b, w_q_x, w_kv_x, w_o_x, w_up, w_down) = (
        w.astype(BF16) for w in (w_qkv_a, w_o_a, w_qkv_b, w_o_b, w_q_x, w_kv_x, w_o_x, w_up, w_down))
    g_mix, g_cross, g_mem, g_mlp = (g.astype(F32)[:, None] for g in (g_mix, g_cross, g_mem, g_mlp))
    qk_gain_a = _qk_gains(q_norm_a, k_norm_a)
    qk_gain_b = _qk_gains(q_norm_b, k_norm_b)
    k_gain_x = jnp.tile(k_norm_x.astype(F32), (1, X_HEADS))[:, None]
    q_gain_x = (q_norm_x.astype(F32) * ATTN_SCALE)[:, None]

    def cross_args(i):
        kv = norm_matmul(mem, g_mem, i, w_kv_x, i, k_gain_x, i,
                         n=2 * X_HEADS * HEAD_DIM, tm=N_SEQS * N_MEM, tn=X_HEADS * HEAD_DIM,
                         name=f"kv_x{i}").reshape(N_SEQS * N_MEM, 2 * X_HEADS * HEAD_DIM)
        return (g_cross, w_q_x, q_gain_x, kv, kv, w_o_x)

    for i in range(DEPTH):
        li = i // 2
        if i % 2 == 0:
            qkv = norm_matmul(x, g_mix, i, w_qkv_a, li, qk_gain_a, li,
                              n=3 * d, tm=QKV_ROWS,
                              tn=QKV_COLS_PAIRED if isinstance(x, tuple) else QKV_COLS,
                              name=f"qkv_a{i}")
            o = neighborhood_attention(qkv.reshape(N_TOK, 3 * d), na_bias, li, name=f"na_attn{i}")
            x = project_cross_residual(o, w_o_a, li, x, cross_args(i), i, tm=PROJ_ROWS,
                                       name=f"wo_cross{i}")
        else:
            outs, lses = [], []
            for g, (_, dil) in enumerate(DIL_CONFIGS):
                qkv = norm_matmul(x, g_mix, i, w_qkv_b, li, qk_gain_b, li * N_GROUPS + g,
                                  n=3 * d, col0=g * 3 * d, tm=QKV_ROWS, tn=QKV_COLS, dil=dil,
                                  name=f"qkv_b{i}_{g}")
                o_g, lse_g = dilated_group_attention(qkv, dil_bias, group=g, name=f"dil_attn{i}_{g}")
                outs.append(o_g)
                lses.append(lse_g)
            x = merge_project_cross_residual(outs, lses, w_o_b, li, x, cross_args(i), i,
                                             tm=MERGE_ROWS, name=f"wo_cross{i}")
        x = mlp_residual(x, g_mlp, w_up, w_down, i, tm=MLP_ROWS, tf=MLP_CHUNK,
                         split_groups=(i == DEPTH - 1), name=f"mlp{i}")

    y_prompt, y_sample = x
    return (y_prompt.reshape(1, SEQ, d), y_sample.reshape(DEC_BATCH, DEC_SEQ, d))
```

```python
import functools
import math

import numpy as np
import jax
import jax.numpy as jnp
from jax import lax
from jax.experimental import pallas as pl
from jax.experimental.pallas import tpu as pltpu

D_MODEL = 2048
SEQ = 8192
DEPTH = 4
DEC_BATCH = 4
DEC_SEQ = 2048
N_TOK = SEQ + DEC_BATCH * DEC_SEQ
N_SEQS = 1 + DEC_BATCH

HEAD_DIM = 128
N_HEADS = D_MODEL // HEAD_DIM
NA_WIN_ROWS = 8
NA_WIN_COLS = 16
GRID_W = 64
DIL_CONFIGS = ((128, 1), (512, 4), (2048, 16))
N_GROUPS = len(DIL_CONFIGS)
DIL_RADIUS = 64
T5_BUCKETS = 32
T5_MAX_DIST = 1024
X_HEADS = 4
N_MEM = 256
D_FF = 4 * D_MODEL
RMS_EPS = 1e-6
ATTN_SCALE = 1.0 / math.sqrt(HEAD_DIM)
NEG_INF = float("-inf")

F32 = jnp.float32
BF16 = jnp.bfloat16
NT_DIMS = (((1,), (1,)), ((), ()))

VMEM_LIMIT_BYTES = 60 * 1024 * 1024

NA_QROWS = 4
NA_BLK = NA_QROWS * GRID_W
NA_KBLKS = 3
NA_NBP = SEQ // NA_BLK
NA_NBS = DEC_SEQ // NA_BLK

DIL_BQ = 128
DIL_KW = DIL_BQ + 2 * DIL_RADIUS
DIL_STEP_ROWS = 512

QKV_ROWS, QKV_COLS = 1024, 2048
QKV_COLS_PAIRED = 1024
PROJ_ROWS = 512
MERGE_ROWS = 256
MLP_ROWS, MLP_CHUNK = 512, 2048
MLP_CHUNK_SPLIT = 1024

MATMUL_CHUNK = 256
DEINTERLEAVE_STRIDE = 4
SLAB_RING = 4


def _params(*sem):
    return pltpu.CompilerParams(dimension_semantics=sem, vmem_limit_bytes=VMEM_LIMIT_BYTES)


def _layer_spec(block, layer, index_map, resident=False):
    return pl.BlockSpec((None,) + block, lambda *ids: (layer,) + index_map(*ids),
                        pipeline_mode=pl.Buffered(1) if resident else None)


def _group_row_specs(tm, width):
    tp = SEQ // tm
    return tp, [pl.BlockSpec((tm, width), lambda i, *_: (jnp.minimum(i, tp - 1), 0)),
                pl.BlockSpec((tm, width), lambda i, *_: (jnp.maximum(i - tp, 0), 0))]


def _norm_rows(x, g):
    ms = jnp.mean(x * x, axis=-1, keepdims=True)
    return x * lax.rsqrt(ms + RMS_EPS) * g


def _head_slice(h):
    return slice(h * HEAD_DIM, (h + 1) * HEAD_DIM)


def _softmax_pv(logits, values, ones=None):
    m = functools.reduce(jnp.maximum, [jnp.max(s, axis=-1, keepdims=True) for s in logits])
    if ones is None:
        probs = [jnp.exp(s - m) for s in logits]
        den = functools.reduce(jnp.add, [jnp.sum(p, axis=-1, keepdims=True) for p in probs])
        acc = functools.reduce(jnp.add, [jnp.dot(p.astype(BF16), v, preferred_element_type=F32)
                                         for p, v in zip(probs, values)])
        return acc / den, m, den
    res = functools.reduce(jnp.add, [
        jnp.dot(jnp.exp((s - m).astype(BF16)), jnp.concatenate([v, ones], axis=1),
                preferred_element_type=F32)
        for s, v in zip(logits, values)])
    den = res[:, HEAD_DIM:]
    return res[:, :HEAD_DIM] / den, m, den[:, :1]


def _norm_matmul_body(*refs, n_norm_blocks, n_blocks, tm, tn, dil, prompt_tiles):
    n_x = 1 if prompt_tiles is None else 2
    x_refs = refs[:n_x]
    g_ref, w_ref, hg_ref, o_ref, xn_ref = refs[n_x:n_x + 5]
    slab_refs = refs[n_x + 5:]
    i = pl.program_id(0)
    j = pl.program_id(1)

    def normalise(x_ref):
        xn_ref[...] = _norm_rows(x_ref[...], g_ref[...]).astype(BF16)

    if prompt_tiles is None:
        pl.when(j == 0)(functools.partial(normalise, x_refs[0]))
    else:
        pl.when((j == 0) & (i < prompt_tiles))(functools.partial(normalise, x_refs[0]))
        pl.when((j == 0) & (i >= prompt_tiles))(functools.partial(normalise, x_refs[1]))

    def run(head_norm):
        for c in range(tn // MATMUL_CHUNK):
            acc = jnp.dot(xn_ref[...], w_ref[:, c * MATMUL_CHUNK:(c + 1) * MATMUL_CHUNK],
                          preferred_element_type=F32)
            for s in range(MATMUL_CHUNK // HEAD_DIM):
                slab = c * (MATMUL_CHUNK // HEAD_DIM) + s
                sl = _head_slice(slab)
                piece = acc[:, _head_slice(s)]
                if head_norm:
                    piece = _norm_rows(piece, hg_ref[:, sl])
                if dil == 1:
                    o_ref[0, :, sl] = piece.astype(o_ref.dtype)
                    continue
                slab_ref = slab_refs[0]
                slot = slab % SLAB_RING
                slab_ref[slot] = piece
                if dil == DEINTERLEAVE_STRIDE:
                    for r in range(dil):
                        rows = slab_ref[slot, pl.ds(r, tm // dil, stride=dil), :]
                        o_ref[r, :, sl] = rows.astype(o_ref.dtype)
                else:
                    quarter_ref = slab_refs[1]
                    outer = dil // DEINTERLEAVE_STRIDE
                    for b in range(DEINTERLEAVE_STRIDE):
                        quarter_ref[slot, b] = slab_ref[
                            slot, pl.ds(b, tm // DEINTERLEAVE_STRIDE, stride=DEINTERLEAVE_STRIDE), :]
                        for a in range(outer):
                            rows = quarter_ref[slot, b, pl.ds(a, tm // dil, stride=outer), :]
                            o_ref[a * DEINTERLEAVE_STRIDE + b, :, sl] = rows.astype(o_ref.dtype)

    if n_norm_blocks == n_blocks:
        run(True)
    else:
        pl.when(j < n_norm_blocks)(functools.partial(run, True))
        pl.when(j >= n_norm_blocks)(functools.partial(run, False))


def norm_matmul(x, gains, gain_idx, w_stack, layer, head_gains, head_gain_idx, *, n, tm, tn, name,
                col0=0, dil=1):
    pair = isinstance(x, tuple)
    m = sum(part.shape[0] for part in x) if pair else x.shape[0]
    k = w_stack.shape[1]
    n_norm = head_gains.shape[2]
    n_blocks = n // tn
    n_norm_blocks = n_norm // tn
    assert m % tm == 0 and n % tn == 0 and n_norm % tn == 0 and n_norm_blocks >= 1 and col0 % tn == 0
    assert tn % MATMUL_CHUNK == 0 and tm % (16 * dil) == 0
    if pair:
        prompt_tiles, x_specs = _group_row_specs(tm, k)
        x_args = list(x)
    else:
        prompt_tiles, x_specs, x_args = None, [pl.BlockSpec((tm, k), lambda i, j: (i, 0))], [x]
    body = functools.partial(_norm_matmul_body, n_norm_blocks=n_norm_blocks, n_blocks=n_blocks,
                             tm=tm, tn=tn, dil=dil, prompt_tiles=prompt_tiles)
    scratch = [pltpu.VMEM((tm, k), BF16)]
    if dil > 1:
        assert dil % DEINTERLEAVE_STRIDE == 0 and dil // DEINTERLEAVE_STRIDE <= DEINTERLEAVE_STRIDE
        scratch.append(pltpu.VMEM((SLAB_RING, tm, HEAD_DIM), F32))
    if dil > DEINTERLEAVE_STRIDE:
        scratch.append(pltpu.VMEM((SLAB_RING, DEINTERLEAVE_STRIDE, tm // DEINTERLEAVE_STRIDE,
                                   HEAD_DIM), F32))
    return pl.pallas_call(
        body,
        grid=(m // tm, n_blocks),
        in_specs=x_specs + [
            _layer_spec((1, k), gain_idx, lambda i, j: (0, 0)),
            _layer_spec((k, tn), layer, lambda i, j: (0, col0 // tn + j)),
            _layer_spec((1, tn), head_gain_idx, lambda i, j: (0, jnp.minimum(j, n_norm_blocks - 1))),
        ],
        out_specs=pl.BlockSpec((dil, tm // dil, tn), lambda i, j: (0, i, j)),
        out_shape=jax.ShapeDtypeStruct((dil, m // dil, n), BF16),
        scratch_shapes=scratch,
        compiler_params=_params("arbitrary" if pair else "parallel", "arbitrary"),
        name=name,
    )(*x_args, gains, w_stack, head_gains)


def _project_cross_body(a_ref, w_ref, *refs, prompt_tiles):
    o_ref = refs[-1]
    if prompt_tiles is None:
        res, cross_refs = refs[0][...], refs[1:-1]
    else:
        res = jnp.where(pl.program_id(0) < prompt_tiles, refs[0][...], refs[1][...])
        cross_refs = refs[2:-1]
    x = res + jnp.dot(a_ref[...], w_ref[...], preferred_element_type=F32)
    o_ref[...] = _cross_residual(x, *cross_refs)


def project_cross_residual(a, w_stack, layer, res, cross_args, cross_layer, *, tm, name):
    m, k = a.shape
    n = w_stack.shape[2]
    assert m % tm == 0
    if isinstance(res, tuple):
        prompt_tiles, res_specs = _group_row_specs(tm, n)
        res_args = list(res)
    else:
        prompt_tiles, res_specs, res_args = None, [pl.BlockSpec((tm, n), lambda i: (i, 0))], [res]
    return pl.pallas_call(
        functools.partial(_project_cross_body, prompt_tiles=prompt_tiles),
        grid=(m // tm,),
        in_specs=[pl.BlockSpec((tm, k), lambda i: (i, 0)),
                  _layer_spec((k, n), layer, lambda i: (0, 0), resident=True)] + res_specs
        + _cross_specs(tm, cross_layer),
        out_specs=pl.BlockSpec((tm, n), lambda i: (i, 0)),
        out_shape=jax.ShapeDtypeStruct((m, n), F32),
        compiler_params=_params("arbitrary" if prompt_tiles is not None else "parallel"),
        name=name,
    )(a, w_stack, *res_args, *cross_args)


def _mlp_body(x_ref, g_ref, wu_ref, wd_ref, *refs, prompt_tiles):
    xn_ref = refs[-1]

    def step(o_ref):
        @pl.when(pl.program_id(1) == 0)
        def _():
            x = x_ref[...]
            xn_ref[...] = _norm_rows(x, g_ref[...]).astype(BF16)
            o_ref[...] = x

        h = jnp.dot(xn_ref[...], wu_ref[...], preferred_element_type=F32)
        h = jnp.maximum(h, 0.0)
        o_ref[...] += jnp.dot((h * h).astype(BF16), wd_ref[...], preferred_element_type=F32)

    if prompt_tiles is None:
        step(refs[0])
    else:
        i = pl.program_id(0)
        pl.when(i < prompt_tiles)(functools.partial(step, refs[0]))
        pl.when(i >= prompt_tiles)(functools.partial(step, refs[1]))


def mlp_residual(x, gains, w_up, w_down, layer, *, tm, tf, name, split_groups=False):
    m, d = x.shape
    ff = w_up.shape[2]
    assert m % tm == 0 and ff % tf == 0 and SEQ % tm == 0
    if split_groups:
        tp = SEQ // tm
        out_specs = [pl.BlockSpec((tm, d), lambda i, f: (jnp.minimum(i, tp - 1), 0)),
                     pl.BlockSpec((tm, d), lambda i, f: (jnp.maximum(i - tp, 0), 0))]
        out_shape = [jax.ShapeDtypeStruct((SEQ, d), F32), jax.ShapeDtypeStruct((m - SEQ, d), F32)]
    else:
        tp = None
        out_specs = pl.BlockSpec((tm, d), lambda i, f: (i, 0))
        out_shape = jax.ShapeDtypeStruct((m, d), F32)
    return pl.pallas_call(
        functools.partial(_mlp_body, prompt_tiles=tp),
        grid=(m // tm, ff // tf),
        in_specs=[
            pl.BlockSpec((tm, d), lambda i, f: (i, 0)),
            _layer_spec((1, d), layer, lambda i, f: (0, 0)),
            _layer_spec((d, tf), layer, lambda i, f: (0, f)),
            _layer_spec((tf, d), layer, lambda i, f: (f, 0)),
        ],
        out_specs=out_specs,
        out_shape=out_shape,
        scratch_shapes=[pltpu.VMEM((tm, d), BF16)],
        compiler_params=_params("arbitrary" if split_groups else "parallel", "arbitrary"),
        name=name,
    )(x, gains, w_up, w_down)


def _cross_residual(x, g_ref, wq_ref, qn_ref, k_ref, v_ref, wo_ref):
    xn = _norm_rows(x, g_ref[...]).astype(BF16)
    q = jnp.dot(xn, wq_ref[...], preferred_element_type=F32)
    heads = []
    for h in range(X_HEADS):
        sl = _head_slice(h)
        qh = _norm_rows(q[:, sl], qn_ref[...]).astype(BF16)
        s = lax.dot_general(qh, k_ref[:, sl], NT_DIMS, preferred_element_type=F32)
        oh, _, _ = _softmax_pv([s], [v_ref[:, sl]])
        heads.append(oh.astype(BF16))
    o = jnp.concatenate(heads, axis=-1)
    return x + jnp.dot(o, wo_ref[...], preferred_element_type=F32)


def _cross_specs(tm, layer, tile_of=lambda i: i):
    d = D_MODEL
    dx = X_HEADS * HEAD_DIM
    assert SEQ % tm == 0 and DEC_SEQ % tm == 0
    tiles_p = SEQ // tm
    tiles_s = DEC_SEQ // tm

    def mem_seq(i):
        return jnp.where(i < tiles_p, 0, 1 + (i - tiles_p) // tiles_s)

    return [
        _layer_spec((1, d), layer, lambda i: (0, 0)),
        _layer_spec((d, dx), layer, lambda i: (0, 0), resident=True),
        _layer_spec((1, HEAD_DIM), layer, lambda i: (0, 0)),
        pl.BlockSpec((N_MEM, dx), lambda i: (mem_seq(tile_of(i)), 0)),
        pl.BlockSpec((N_MEM, dx), lambda i: (mem_seq(tile_of(i)), 1)),
        _layer_spec((dx, d), layer, lambda i: (0, 0), resident=True),
    ]


def _na_block_info(i):
    in_prompt = i < NA_NBP
    local = jnp.where(in_prompt, i, (i - NA_NBP) % NA_NBS)
    nblk = jnp.where(in_prompt, NA_NBP, NA_NBS)
    base = jnp.where(in_prompt, 0, NA_NBP + ((i - NA_NBP) // NA_NBS) * NA_NBS)
    kblk = base + jnp.clip(local - 1, 0, nblk - NA_KBLKS)
    placement = jnp.where(local == 0, 0, jnp.where(local == nblk - 1, 2, 1))
    return kblk, placement


def _na_body(q_ref, k0_ref, k1_ref, k2_ref, v0_ref, v1_ref, v2_ref, b_ref, o_ref):
    k_refs = (k0_ref, k1_ref, k2_ref)
    v_refs = (v0_ref, v1_ref, v2_ref)
    ones = jnp.ones((NA_BLK, HEAD_DIM), BF16)
    for h in range(N_HEADS):
        sl = _head_slice(h)
        q = q_ref[:, sl]
        logits = []
        for c in range(NA_KBLKS):
            s = lax.dot_general(q, k_refs[c][:, sl], NT_DIMS, preferred_element_type=F32)
            logits.append(s + b_ref[0, h, :, c * NA_BLK:(c + 1) * NA_BLK])
        o, _, _ = _softmax_pv(logits, [v_ref[:, sl] for v_ref in v_refs], ones)
        o_ref[:, sl] = o.astype(o_ref.dtype)


def neighborhood_attention(qkv, bias_tables, layer, *, name):
    d = D_MODEL

    def kv_spec(c, part):
        return pl.BlockSpec((NA_BLK, d), lambda i: (_na_block_info(i)[0] + c, part))

    return pl.pallas_call(
        _na_body,
        grid=(N_TOK // NA_BLK,),
        in_specs=[pl.BlockSpec((NA_BLK, d), lambda i: (i, 0))]
        + [kv_spec(c, 1) for c in range(NA_KBLKS)]
        + [kv_spec(c, 2) for c in range(NA_KBLKS)]
        + [pl.BlockSpec((None, 1, N_HEADS, NA_BLK, NA_KBLKS * NA_BLK),
                        lambda i: (layer, _na_block_info(i)[1], 0, 0, 0))],
        out_specs=pl.BlockSpec((NA_BLK, d), lambda i: (i, 0)),
        out_shape=jax.ShapeDtypeStruct((N_TOK, d), BF16),
        compiler_params=_params("parallel"),
        name=name,
    )(*([qkv] * (1 + 2 * NA_KBLKS)), bias_tables)


def _toeplitz(v, n_rows, n_cols):
    period = v.shape[-1]
    assert period >= n_rows + n_cols - 1
    flat = jnp.tile(v, (1,) * (v.ndim - 1) + (n_rows,))[..., :n_rows * (period - 1)]
    return flat.reshape(v.shape[:-1] + (n_rows, period - 1))[..., :n_cols]


def na_bias_tables(rpb):
    n_layers = rpb.shape[0]
    n_krows = NA_KBLKS * NA_QROWS
    n_dr = 2 * NA_WIN_ROWS - 1
    assert n_krows - NA_WIN_ROWS == NA_QROWS
    c = np.arange(GRID_W)
    cs = np.clip(c - NA_WIN_COLS // 2, 0, GRID_W - NA_WIN_COLS)
    col_ok = (c[None, :] >= cs[:, None]) & (c[None, :] < cs[:, None] + NA_WIN_COLS)
    dist = np.arange(2 * GRID_W)
    dist = np.where(dist < GRID_W, dist, dist - 2 * GRID_W)
    by_dist = rpb.astype(F32)[..., np.clip(dist, -(NA_WIN_COLS - 1), NA_WIN_COLS - 1) + (NA_WIN_COLS - 1)]
    tiles = jnp.where(col_ok, _toeplitz(by_dist, GRID_W, GRID_W), NEG_INF)
    side_by_side = tiles.transpose(0, 1, 3, 2, 4).reshape(n_layers, N_HEADS, GRID_W, n_dr * GRID_W)

    def strip(first_dr, lead):
        window = side_by_side[..., first_dr * GRID_W:(first_dr + NA_WIN_ROWS) * GRID_W]
        pad = [(0, 0)] * 3 + [(lead * GRID_W, (n_krows - NA_WIN_ROWS - lead) * GRID_W)]
        return jnp.pad(window, pad, constant_values=NEG_INF)

    placements = (
        [strip(7 - rq, 0) for rq in range(NA_QROWS)],
        [strip(3, rq) for rq in range(NA_QROWS)],
        [strip(3 - rq, 4) for rq in range(NA_QROWS)],
    )
    return jnp.stack([jnp.concatenate(strips, axis=2) for strips in placements], axis=1)


def _dil_body(q_ref, k_ref, v_ref, kp_ref, kn_ref, vp_ref, vn_ref, b_ref, o_ref, lse_ref,
              kext_ref, vext_ref, *, dil, tl, planes):
    lse_ref[...] = jnp.zeros_like(lse_ref)

    row0 = pl.program_id(1) * tl
    rows_prompt = SEQ // dil
    len_sample = DEC_SEQ // dil
    in_prompt = row0 < rows_prompt
    pos0 = jnp.where(in_prompt, row0, (row0 - rows_prompt) % len_sample)
    seq_len = jnp.where(in_prompt, rows_prompt, len_sample)

    for p in range(planes):
        for ext_ref, prev_ref, cur_ref, next_ref in ((kext_ref, kp_ref, k_ref, kn_ref),
                                                     (vext_ref, vp_ref, v_ref, vn_ref)):
            ext_ref[p, 0:DIL_RADIUS] = prev_ref[p]
            ext_ref[p, DIL_RADIUS:DIL_RADIUS + tl] = cur_ref[p]
            ext_ref[p, DIL_RADIUS + tl:] = next_ref[p]

        for qs in range(0, tl, DIL_BQ):
            rows = slice(qs, qs + DIL_BQ)
            win = slice(qs, qs + DIL_KW)
            kpos = pos0 + qs - DIL_RADIUS + lax.broadcasted_iota(jnp.int32, (1, DIL_KW), 1)
            edge = jnp.where((kpos >= 0) & (kpos < seq_len), 0.0, NEG_INF).astype(F32)
            for h in range(N_HEADS):
                sl = _head_slice(h)
                s = lax.dot_general(q_ref[p, rows, sl], kext_ref[p, win, sl], NT_DIMS,
                                    preferred_element_type=F32)
                s = s + b_ref[0, h] + edge
                o, m, den = _softmax_pv([s], [vext_ref[p, win, sl]])
                o_ref[p, rows, sl] = o
                lse_ref[p, rows, h:h + 1] = m + jnp.log(den)


def dilated_group_attention(qkv, bias_table, *, group, name):
    d = D_MODEL
    dil = DIL_CONFIGS[group][1]
    assert DIL_CONFIGS[group][0] // (2 * dil) == DIL_RADIUS
    rows = N_TOK // dil
    tl = min(DIL_STEP_ROWS, DEC_SEQ // dil)
    planes = min(dil, DIL_STEP_ROWS // tl)
    assert (SEQ // dil) % tl == 0 and (DEC_SEQ // dil) % tl == 0 and tl % DIL_BQ == 0
    assert dil % planes == 0
    halo_per_tile = tl // DIL_RADIUS
    n_halo = rows // DIL_RADIUS

    def main_spec(part):
        return pl.BlockSpec((planes, tl, d), lambda r, t: (r, t, part))

    def prev_spec(part):
        return pl.BlockSpec((planes, DIL_RADIUS, d),
                            lambda r, t: (r, jnp.maximum(t * halo_per_tile - 1, 0), part))

    def next_spec(part):
        return pl.BlockSpec((planes, DIL_RADIUS, d),
                            lambda r, t: (r, jnp.minimum((t + 1) * halo_per_tile, n_halo - 1), part))

    return pl.pallas_call(
        functools.partial(_dil_body, dil=dil, tl=tl, planes=planes),
        grid=(dil // planes, rows // tl),
        in_specs=[main_spec(0), main_spec(1), main_spec(2),
                  prev_spec(1), next_spec(1), prev_spec(2), next_spec(2),
                  pl.BlockSpec((1, N_HEADS, DIL_BQ, DIL_KW), lambda r, t: (group, 0, 0, 0))],
        out_specs=[pl.BlockSpec((planes, tl, d), lambda r, t: (r, t, 0)),
                   pl.BlockSpec((planes, tl, HEAD_DIM), lambda r, t: (r, t, 0))],
        out_shape=[jax.ShapeDtypeStruct((dil, rows, d), F32),
                   jax.ShapeDtypeStruct((dil, rows, HEAD_DIM), F32)],
        scratch_shapes=[pltpu.VMEM((planes, tl + 2 * DIL_RADIUS, d), BF16),
                        pltpu.VMEM((planes, tl + 2 * DIL_RADIUS, d), BF16)],
        compiler_params=_params("parallel", "parallel"),
        name=name,
    )(*([qkv] * 7), bias_table)


def _t5_bucket(rel):
    nb = T5_BUCKETS // 2
    max_exact = nb // 2
    ret = jnp.where(rel > 0, nb, 0)
    n = jnp.abs(rel)
    n_f = jnp.maximum(n, 1).astype(F32)
    large = max_exact + (jnp.log(n_f / max_exact) / math.log(T5_MAX_DIST / max_exact)
                         * (nb - max_exact)).astype(jnp.int32)
    large = jnp.minimum(large, nb - 1)
    return ret + jnp.where(n < max_exact, n, large)


def dilated_bias_table(t5_table):
    period = DIL_BQ + DIL_KW
    rel = np.arange(-DIL_RADIUS, DIL_RADIUS + 1)
    pad = jnp.full((N_HEADS, period - rel.size), NEG_INF, F32)
    tables = []
    for g, (_, dil) in enumerate(DIL_CONFIGS):
        band = t5_table[:, g].astype(F32)[_t5_bucket(jnp.asarray(rel * dil, jnp.int32))]
        tables.append(_toeplitz(jnp.concatenate([band.T, pad], axis=-1), DIL_BQ, DIL_KW))
    return jnp.stack(tables)


def _merge_project_body(*refs, tm):
    o_refs = refs[:N_GROUPS]
    l_refs = refs[N_GROUPS:2 * N_GROUPS]
    w_ref, r_ref = refs[2 * N_GROUPS:2 * N_GROUPS + 2]
    cross_refs = refs[2 * N_GROUPS + 2:-5]
    out_ref, ltok_ref, otok_ref = refs[-5:-2]
    merged_refs = refs[-2:]
    dils = [dil for _, dil in DIL_CONFIGS]
    assert dils[0] == 1
    i = pl.program_id(0)

    @pl.when(i == 0)
    def _():
        merged_refs[1][...] = jnp.zeros_like(merged_refs[1])

    def to_token_order(dst_ref, idx, src, dil, r):
        if dil == 1:
            dst_ref[idx] = src
        else:
            dst_ref[idx + (pl.ds(r, tm // dil, stride=dil), slice(None))] = src

    def step(prev_ref, cur_ref):
        x = r_ref[...] + jnp.dot(prev_ref[...], w_ref[...], preferred_element_type=F32)
        out_ref[...] = _cross_residual(x, *cross_refs)

        for g, dil in enumerate(dils):
            for r in range(dil):
                to_token_order(ltok_ref, (g,), l_refs[g][r], dil, r)
        lses = [ltok_ref[g] for g in range(N_GROUPS)]
        top = functools.reduce(jnp.maximum, lses)
        wts = [jnp.exp(l - top) for l in lses]
        z = functools.reduce(jnp.add, wts)
        wts = [w / z for w in wts]
        for g, dil in enumerate(dils):
            if dil == 1:
                continue
            for r in range(dil):
                for h in range(N_HEADS):
                    to_token_order(otok_ref, (g - 1, h), o_refs[g][r, :, _head_slice(h)], dil, r)
        for h in range(N_HEADS):
            sl = _head_slice(h)
            parts = [o_refs[0][0, :, sl]] + [otok_ref[g - 1, h] for g in range(1, N_GROUPS)]
            mix = functools.reduce(jnp.add, [p * wts[g][:, h:h + 1] for g, p in enumerate(parts)])
            cur_ref[:, sl] = mix.astype(BF16)

    pl.when(i % 2 == 0)(functools.partial(step, merged_refs[1], merged_refs[0]))
    pl.when(i % 2 == 1)(functools.partial(step, merged_refs[0], merged_refs[1]))


def merge_project_cross_residual(outs, lses, w_stack, layer, res, cross_args, cross_layer, *, tm, name):
    m, d = res.shape
    dils = [dil for _, dil in DIL_CONFIGS]
    assert m % tm == 0 and all(tm % (8 * dil) == 0 for dil in dils)
    n_tiles = m // tm

    def merging(i):
        return jnp.minimum(i, n_tiles - 1)

    def projecting(i):
        return jnp.maximum(i - 1, 0)

    in_specs = [pl.BlockSpec((dil, tm // dil, d), lambda i: (0, merging(i), 0)) for dil in dils]
    in_specs += [pl.BlockSpec((dil, tm // dil, HEAD_DIM), lambda i: (0, merging(i), 0)) for dil in dils]
    in_specs += [_layer_spec((d, d), layer, lambda i: (0, 0), resident=True),
                 pl.BlockSpec((tm, d), lambda i: (projecting(i), 0))]
    in_specs += _cross_specs(tm, cross_layer, tile_of=projecting)
    return pl.pallas_call(
        functools.partial(_merge_project_body, tm=tm),
        grid=(n_tiles + 1,),
        in_specs=in_specs,
        out_specs=pl.BlockSpec((tm, d), lambda i: (projecting(i), 0)),
        out_shape=jax.ShapeDtypeStruct((m, d), F32),
        scratch_shapes=[pltpu.VMEM((N_GROUPS, tm, HEAD_DIM), F32),
                        pltpu.VMEM((N_GROUPS - 1, N_HEADS, tm, HEAD_DIM), F32),
                        pltpu.VMEM((tm, d), BF16),
                        pltpu.VMEM((tm, d), BF16)],
        compiler_params=_params("arbitrary"),
        name=name,
    )(*outs, *lses, w_stack, res, *cross_args)


def _qk_gains(q_gains, k_gains):
    tiled = [jnp.tile(g.astype(F32).reshape(-1, HEAD_DIM), (1, N_HEADS)) for g in (q_gains, k_gains)]
    return jnp.concatenate([tiled[0] * ATTN_SCALE, tiled[1]], axis=1)[:, None]


def kernel(x_prompt, x_sample, mem_prompt, mem_sample, g_mix, g_cross, g_mem, g_mlp, w_qkv_a, q_norm_a, k_norm_a, rpb_a, w_o_a, w_qkv_b, q_norm_b, k_norm_b, t5_table, w_o_b, w_q_x, w_kv_x, q_norm_x, k_norm_x, w_o_x, w_up, w_down):
    d = D_MODEL
    x = (x_prompt.reshape(SEQ, d), x_sample.reshape(DEC_BATCH * DEC_SEQ, d))
    mem = jnp.concatenate([mem_prompt.reshape(N_MEM, d), mem_sample.reshape(DEC_BATCH * N_MEM, d)])
    dil_bias = dilated_bias_table(t5_table)
    na_bias = na_bias_tables(rpb_a)
    (w_qkv_a, w_o_a, w_qkv_b, w_o_b, w_q_x, w_kv_x, w_o_x, w_up, w_down) = (
        w.astype(BF16) for w in (w_qkv_a, w_o_a, w_qkv_b, w_o_b, w_q_x, w_kv_x, w_o_x, w_up, w_down))
    g_mix, g_cross, g_mem, g_mlp = (g.astype(F32)[:, None] for g in (g_mix, g_cross, g_mem, g_mlp))
    qk_gain_a = _qk_gains(q_norm_a, k_norm_a)
    qk_gain_b = _qk_gains(q_norm_b, k_norm_b)
    k_gain_x = jnp.tile(k_norm_x.astype(F32), (1, X_HEADS))[:, None]
    q_gain_x = (q_norm_x.astype(F32) * ATTN_SCALE)[:, None]

    def cross_args(i):
        kv = norm_matmul(mem, g_mem, i, w_kv_x, i, k_gain_x, i,
                         n=2 * X_HEADS * HEAD_DIM, tm=N_SEQS * N_MEM, tn=X_HEADS * HEAD_DIM,
                         name=f"kv_x{i}").reshape(N_SEQS * N_MEM, 2 * X_HEADS * HEAD_DIM)
        return (g_cross, w_q_x, q_gain_x, kv, kv, w_o_x)

    for i in range(DEPTH):
        li = i // 2
        if i % 2 == 0:
            qkv = norm_matmul(x, g_mix, i, w_qkv_a, li, qk_gain_a, li,
                              n=3 * d, tm=QKV_ROWS,
                              tn=QKV_COLS_PAIRED if isinstance(x, tuple) else QKV_COLS,
                              name=f"qkv_a{i}")
            o = neighborhood_attention(qkv.reshape(N_TOK, 3 * d), na_bias, li, name=f"na_attn{i}")
            x = project_cross_residual(o, w_o_a, li, x, cross_args(i), i, tm=PROJ_ROWS,
                                       name=f"wo_cross{i}")
        else:
            outs, lses = [], []
            for g, (_, dil) in enumerate(DIL_CONFIGS):
                qkv = norm_matmul(x, g_mix, i, w_qkv_b, li, qk_gain_b, li * N_GROUPS + g,
                                  n=3 * d, col0=g * 3 * d, tm=QKV_ROWS, tn=QKV_COLS, dil=dil,
                                  name=f"qkv_b{i}_{g}")
                o_g, lse_g = dilated_group_attention(qkv, dil_bias, group=g, name=f"dil_attn{i}_{g}")
                outs.append(o_g)
                lses.append(lse_g)
            x = merge_project_cross_residual(outs, lses, w_o_b, li, x, cross_args(i), i,
                                             tm=MERGE_ROWS, name=f"wo_cross{i}")
        last = i == DEPTH - 1
        x = mlp_residual(x, g_mlp, w_up, w_down, i, tm=MLP_ROWS,
                         tf=MLP_CHUNK_SPLIT if last else MLP_CHUNK, split_groups=last, name=f"mlp{i}")

    y_prompt, y_sample = x
    return (y_prompt.reshape(1, SEQ, d), y_sample.reshape(DEC_BATCH, DEC_SEQ, d))
```

```python
import functools
import math

import numpy as np
import jax
import jax.numpy as jnp
from jax import lax
from jax.experimental import pallas as pl
from jax.experimental.pallas import tpu as pltpu

D_MODEL = 2048
SEQ = 8192
DEPTH = 4
DEC_BATCH = 4
DEC_SEQ = 2048
N_TOK = SEQ + DEC_BATCH * DEC_SEQ
N_SEQS = 1 + DEC_BATCH

HEAD_DIM = 128
N_HEADS = D_MODEL // HEAD_DIM
NA_WIN_ROWS = 8
NA_WIN_COLS = 16
GRID_W = 64
DIL_CONFIGS = ((128, 1), (512, 4), (2048, 16))
N_GROUPS = len(DIL_CONFIGS)
DIL_RADIUS = 64
T5_BUCKETS = 32
T5_MAX_DIST = 1024
X_HEADS = 4
N_MEM = 256
D_FF = 4 * D_MODEL
RMS_EPS = 1e-6
ATTN_SCALE = 1.0 / math.sqrt(HEAD_DIM)
NEG_INF = float("-inf")

F32 = jnp.float32
BF16 = jnp.bfloat16
NT_DIMS = (((1,), (1,)), ((), ()))

VMEM_LIMIT_BYTES = 60 * 1024 * 1024

NA_QROWS = 4
NA_BLK = NA_QROWS * GRID_W
NA_KBLKS = 3
NA_NBP = SEQ // NA_BLK
NA_NBS = DEC_SEQ // NA_BLK

DIL_BQ = 128
DIL_KW = DIL_BQ + 2 * DIL_RADIUS
DIL_STEP_ROWS = 512

QKV_ROWS, QKV_COLS = 1024, 2048
QKV_COLS_PAIRED = 1024
PROJ_ROWS = 512
MERGE_ROWS = 256
MLP_ROWS, MLP_CHUNK = 512, 2048

MATMUL_CHUNK = 256
DEINTERLEAVE_STRIDE = 4
SLAB_RING = 4


def _params(*sem):
    return pltpu.CompilerParams(dimension_semantics=sem, vmem_limit_bytes=VMEM_LIMIT_BYTES)


def _layer_spec(block, layer, index_map, resident=False):
    return pl.BlockSpec((None,) + block, lambda *ids: (layer,) + index_map(*ids),
                        pipeline_mode=pl.Buffered(1) if resident else None)


def _group_row_specs(tm, width):
    tp = SEQ // tm
    return tp, [pl.BlockSpec((tm, width), lambda i, *_: (jnp.minimum(i, tp - 1), 0)),
                pl.BlockSpec((tm, width), lambda i, *_: (jnp.maximum(i - tp, 0), 0))]


def _norm_rows(x, g):
    ms = jnp.mean(x * x, axis=-1, keepdims=True)
    return x * lax.rsqrt(ms + RMS_EPS) * g


def _head_slice(h):
    return slice(h * HEAD_DIM, (h + 1) * HEAD_DIM)


def _softmax_pv(logits, values, ones=None):
    m = functools.reduce(jnp.maximum, [jnp.max(s, axis=-1, keepdims=True) for s in logits])
    if ones is None:
        probs = [jnp.exp(s - m) for s in logits]
        den = functools.reduce(jnp.add, [jnp.sum(p, axis=-1, keepdims=True) for p in probs])
        acc = functools.reduce(jnp.add, [jnp.dot(p.astype(BF16), v, preferred_element_type=F32)
                                         for p, v in zip(probs, values)])
        return acc / den, m, den
    res = functools.reduce(jnp.add, [
        jnp.dot(jnp.exp((s - m).astype(BF16)), jnp.concatenate([v, ones], axis=1),
                preferred_element_type=F32)
        for s, v in zip(logits, values)])
    den = res[:, HEAD_DIM:]
    return res[:, :HEAD_DIM] / den, m, den[:, :1]


def _norm_matmul_body(*refs, n_norm_blocks, n_blocks, tm, tn, dil, prompt_tiles):
    n_x = 1 if prompt_tiles is None else 2
    x_refs = refs[:n_x]
    g_ref, w_ref, hg_ref, o_ref, xn_ref = refs[n_x:n_x + 5]
    slab_refs = refs[n_x + 5:]
    i = pl.program_id(0)
    j = pl.program_id(1)

    def normalise(x_ref):
        xn_ref[...] = _norm_rows(x_ref[...], g_ref[...]).astype(BF16)

    if prompt_tiles is None:
        pl.when(j == 0)(functools.partial(normalise, x_refs[0]))
    else:
        pl.when((j == 0) & (i < prompt_tiles))(functools.partial(normalise, x_refs[0]))
        pl.when((j == 0) & (i >= prompt_tiles))(functools.partial(normalise, x_refs[1]))

    def run(head_norm):
        for c in range(tn // MATMUL_CHUNK):
            acc = jnp.dot(xn_ref[...], w_ref[:, c * MATMUL_CHUNK:(c + 1) * MATMUL_CHUNK],
                          preferred_element_type=F32)
            for s in range(MATMUL_CHUNK // HEAD_DIM):
                slab = c * (MATMUL_CHUNK // HEAD_DIM) + s
                sl = _head_slice(slab)
                piece = acc[:, _head_slice(s)]
                if head_norm:
                    piece = _norm_rows(piece, hg_ref[:, sl])
                if dil == 1:
                    o_ref[0, :, sl] = piece.astype(o_ref.dtype)
                    continue
                slab_ref = slab_refs[0]
                slot = slab % SLAB_RING
                slab_ref[slot] = piece
                if dil == DEINTERLEAVE_STRIDE:
                    for r in range(dil):
                        rows = slab_ref[slot, pl.ds(r, tm // dil, stride=dil), :]
                        o_ref[r, :, sl] = rows.astype(o_ref.dtype)
                else:
                    quarter_ref = slab_refs[1]
                    outer = dil // DEINTERLEAVE_STRIDE
                    for b in range(DEINTERLEAVE_STRIDE):
                        quarter_ref[slot, b] = slab_ref[
                            slot, pl.ds(b, tm // DEINTERLEAVE_STRIDE, stride=DEINTERLEAVE_STRIDE), :]
                        for a in range(outer):
                            rows = quarter_ref[slot, b, pl.ds(a, tm // dil, stride=outer), :]
                            o_ref[a * DEINTERLEAVE_STRIDE + b, :, sl] = rows.astype(o_ref.dtype)

    if n_norm_blocks == n_blocks:
        run(True)
    else:
        pl.when(j < n_norm_blocks)(functools.partial(run, True))
        pl.when(j >= n_norm_blocks)(functools.partial(run, False))


def norm_matmul(x, gains, gain_idx, w_stack, layer, head_gains, head_gain_idx, *, n, tm, tn, name,
                col0=0, dil=1):
    pair = isinstance(x, tuple)
    m = sum(part.shape[0] for part in x) if pair else x.shape[0]
    k = w_stack.shape[1]
    n_norm = head_gains.shape[2]
    n_blocks = n // tn
    n_norm_blocks = n_norm // tn
    assert m % tm == 0 and n % tn == 0 and n_norm % tn == 0 and n_norm_blocks >= 1 and col0 % tn == 0
    assert tn % MATMUL_CHUNK == 0 and tm % (16 * dil) == 0
    if pair:
        prompt_tiles, x_specs = _group_row_specs(tm, k)
        x_args = list(x)
    else:
        prompt_tiles, x_specs, x_args = None, [pl.BlockSpec((tm, k), lambda i, j: (i, 0))], [x]
    body = functools.partial(_norm_matmul_body, n_norm_blocks=n_norm_blocks, n_blocks=n_blocks,
                             tm=tm, tn=tn, dil=dil, prompt_tiles=prompt_tiles)
    scratch = [pltpu.VMEM((tm, k), BF16)]
    if dil > 1:
        assert dil % DEINTERLEAVE_STRIDE == 0 and dil // DEINTERLEAVE_STRIDE <= DEINTERLEAVE_STRIDE
        scratch.append(pltpu.VMEM((SLAB_RING, tm, HEAD_DIM), F32))
    if dil > DEINTERLEAVE_STRIDE:
        scratch.append(pltpu.VMEM((SLAB_RING, DEINTERLEAVE_STRIDE, tm // DEINTERLEAVE_STRIDE,
                                   HEAD_DIM), F32))
    return pl.pallas_call(
        body,
        grid=(m // tm, n_blocks),
        in_specs=x_specs + [
            _layer_spec((1, k), gain_idx, lambda i, j: (0, 0)),
            _layer_spec((k, tn), layer, lambda i, j: (0, col0 // tn + j)),
            _layer_spec((1, tn), head_gain_idx, lambda i, j: (0, jnp.minimum(j, n_norm_blocks - 1))),
        ],
        out_specs=pl.BlockSpec((dil, tm // dil, tn), lambda i, j: (0, i, j)),
        out_shape=jax.ShapeDtypeStruct((dil, m // dil, n), BF16),
        scratch_shapes=scratch,
        compiler_params=_params("arbitrary" if pair else "parallel", "arbitrary"),
        name=name,
    )(*x_args, gains, w_stack, head_gains)


def _project_cross_body(a_ref, w_ref, *refs, prompt_tiles):
    o_ref = refs[-1]
    if prompt_tiles is None:
        res, cross_refs = refs[0][...], refs[1:-1]
    else:
        res = jnp.where(pl.program_id(0) < prompt_tiles, refs[0][...], refs[1][...])
        cross_refs = refs[2:-1]
    x = res + jnp.dot(a_ref[...], w_ref[...], preferred_element_type=F32)
    o_ref[...] = _cross_residual(x, *cross_refs)


def project_cross_residual(a, w_stack, layer, res, cross_args, cross_layer, *, tm, name):
    m, k = a.shape
    n = w_stack.shape[2]
    assert m % tm == 0
    if isinstance(res, tuple):
        prompt_tiles, res_specs = _group_row_specs(tm, n)
        res_args = list(res)
    else:
        prompt_tiles, res_specs, res_args = None, [pl.BlockSpec((tm, n), lambda i: (i, 0))], [res]
    return pl.pallas_call(
        functools.partial(_project_cross_body, prompt_tiles=prompt_tiles),
        grid=(m // tm,),
        in_specs=[pl.BlockSpec((tm, k), lambda i: (i, 0)),
                  _layer_spec((k, n), layer, lambda i: (0, 0), resident=True)] + res_specs
        + _cross_specs(tm, cross_layer),
        out_specs=pl.BlockSpec((tm, n), lambda i: (i, 0)),
        out_shape=jax.ShapeDtypeStruct((m, n), F32),
        compiler_params=_params("arbitrary" if prompt_tiles is not None else "parallel"),
        name=name,
    )(a, w_stack, *res_args, *cross_args)


def _mlp_body(x_ref, g_ref, wu_ref, wd_ref, o_ref, xn_ref):
    @pl.when(pl.program_id(1) == 0)
    def _():
        x = x_ref[...]
        xn_ref[...] = _norm_rows(x, g_ref[...]).astype(BF16)
        o_ref[...] = x

    h = jnp.dot(xn_ref[...], wu_ref[...], preferred_element_type=F32)
    h = jnp.maximum(h, 0.0)
    o_ref[...] += jnp.dot((h * h).astype(BF16), wd_ref[...], preferred_element_type=F32)


def mlp_residual(x, gains, w_up, w_down, layer, *, tm, tf, name, rows=None):
    first, count = (0, x.shape[0]) if rows is None else rows
    d = x.shape[1]
    ff = w_up.shape[2]
    assert first % tm == 0 and count % tm == 0 and ff % tf == 0
    return pl.pallas_call(
        _mlp_body,
        grid=(count // tm, ff // tf),
        in_specs=[
            pl.BlockSpec((tm, d), lambda i, f: (first // tm + i, 0)),
            _layer_spec((1, d), layer, lambda i, f: (0, 0)),
            _layer_spec((d, tf), layer, lambda i, f: (0, f)),
            _layer_spec((tf, d), layer, lambda i, f: (f, 0)),
        ],
        out_specs=pl.BlockSpec((tm, d), lambda i, f: (i, 0)),
        out_shape=jax.ShapeDtypeStruct((count, d), F32),
        scratch_shapes=[pltpu.VMEM((tm, d), BF16)],
        compiler_params=_params("parallel", "arbitrary"),
        name=name,
    )(x, gains, w_up, w_down)


def _cross_residual(x, g_ref, wq_ref, qn_ref, k_ref, v_ref, wo_ref):
    xn = _norm_rows(x, g_ref[...]).astype(BF16)
    q = jnp.dot(xn, wq_ref[...], preferred_element_type=F32)
    heads = []
    for h in range(X_HEADS):
        sl = _head_slice(h)
        qh = _norm_rows(q[:, sl], qn_ref[...]).astype(BF16)
        s = lax.dot_general(qh, k_ref[:, sl], NT_DIMS, preferred_element_type=F32)
        oh, _, _ = _softmax_pv([s], [v_ref[:, sl]])
        heads.append(oh.astype(BF16))
    o = jnp.concatenate(heads, axis=-1)
    return x + jnp.dot(o, wo_ref[...], preferred_element_type=F32)


def _cross_specs(tm, layer, tile_of=lambda i: i):
    d = D_MODEL
    dx = X_HEADS * HEAD_DIM
    assert SEQ % tm == 0 and DEC_SEQ % tm == 0
    tiles_p = SEQ // tm
    tiles_s = DEC_SEQ // tm

    def mem_seq(i):
        return jnp.where(i < tiles_p, 0, 1 + (i - tiles_p) // tiles_s)

    return [
        _layer_spec((1, d), layer, lambda i: (0, 0)),
        _layer_spec((d, dx), layer, lambda i: (0, 0), resident=True),
        _layer_spec((1, HEAD_DIM), layer, lambda i: (0, 0)),
        pl.BlockSpec((N_MEM, dx), lambda i: (mem_seq(tile_of(i)), 0)),
        pl.BlockSpec((N_MEM, dx), lambda i: (mem_seq(tile_of(i)), 1)),
        _layer_spec((dx, d), layer, lambda i: (0, 0), resident=True),
    ]


def _na_block_info(i):
    in_prompt = i < NA_NBP
    local = jnp.where(in_prompt, i, (i - NA_NBP) % NA_NBS)
    nblk = jnp.where(in_prompt, NA_NBP, NA_NBS)
    base = jnp.where(in_prompt, 0, NA_NBP + ((i - NA_NBP) // NA_NBS) * NA_NBS)
    kblk = base + jnp.clip(local - 1, 0, nblk - NA_KBLKS)
    placement = jnp.where(local == 0, 0, jnp.where(local == nblk - 1, 2, 1))
    return kblk, placement


def _na_body(q_ref, k0_ref, k1_ref, k2_ref, v0_ref, v1_ref, v2_ref, b_ref, o_ref):
    k_refs = (k0_ref, k1_ref, k2_ref)
    v_refs = (v0_ref, v1_ref, v2_ref)
    ones = jnp.ones((NA_BLK, HEAD_DIM), BF16)
    for h in range(N_HEADS):
        sl = _head_slice(h)
        q = q_ref[:, sl]
        logits = []
        for c in range(NA_KBLKS):
            s = lax.dot_general(q, k_refs[c][:, sl], NT_DIMS, preferred_element_type=F32)
            logits.append(s + b_ref[0, h, :, c * NA_BLK:(c + 1) * NA_BLK])
        o, _, _ = _softmax_pv(logits, [v_ref[:, sl] for v_ref in v_refs], ones)
        o_ref[:, sl] = o.astype(o_ref.dtype)


def neighborhood_attention(qkv, bias_tables, layer, *, name):
    d = D_MODEL

    def kv_spec(c, part):
        return pl.BlockSpec((NA_BLK, d), lambda i: (_na_block_info(i)[0] + c, part))

    return pl.pallas_call(
        _na_body,
        grid=(N_TOK // NA_BLK,),
        in_specs=[pl.BlockSpec((NA_BLK, d), lambda i: (i, 0))]
        + [kv_spec(c, 1) for c in range(NA_KBLKS)]
        + [kv_spec(c, 2) for c in range(NA_KBLKS)]
        + [pl.BlockSpec((None, 1, N_HEADS, NA_BLK, NA_KBLKS * NA_BLK),
                        lambda i: (layer, _na_block_info(i)[1], 0, 0, 0))],
        out_specs=pl.BlockSpec((NA_BLK, d), lambda i: (i, 0)),
        out_shape=jax.ShapeDtypeStruct((N_TOK, d), BF16),
        compiler_params=_params("parallel"),
        name=name,
    )(*([qkv] * (1 + 2 * NA_KBLKS)), bias_tables)


def _toeplitz(v, n_rows, n_cols):
    period = v.shape[-1]
    assert period >= n_rows + n_cols - 1
    flat = jnp.tile(v, (1,) * (v.ndim - 1) + (n_rows,))[..., :n_rows * (period - 1)]
    return flat.reshape(v.shape[:-1] + (n_rows, period - 1))[..., :n_cols]


def na_bias_tables(rpb):
    n_layers = rpb.shape[0]
    n_krows = NA_KBLKS * NA_QROWS
    n_dr = 2 * NA_WIN_ROWS - 1
    assert n_krows - NA_WIN_ROWS == NA_QROWS
    c = np.arange(GRID_W)
    cs = np.clip(c - NA_WIN_COLS // 2, 0, GRID_W - NA_WIN_COLS)
    col_ok = (c[None, :] >= cs[:, None]) & (c[None, :] < cs[:, None] + NA_WIN_COLS)
    dist = np.arange(2 * GRID_W)
    dist = np.where(dist < GRID_W, dist, dist - 2 * GRID_W)
    by_dist = rpb.astype(F32)[..., np.clip(dist, -(NA_WIN_COLS - 1), NA_WIN_COLS - 1) + (NA_WIN_COLS - 1)]
    tiles = jnp.where(col_ok, _toeplitz(by_dist, GRID_W, GRID_W), NEG_INF)
    side_by_side = tiles.transpose(0, 1, 3, 2, 4).reshape(n_layers, N_HEADS, GRID_W, n_dr * GRID_W)

    def strip(first_dr, lead):
        window = side_by_side[..., first_dr * GRID_W:(first_dr + NA_WIN_ROWS) * GRID_W]
        pad = [(0, 0)] * 3 + [(lead * GRID_W, (n_krows - NA_WIN_ROWS - lead) * GRID_W)]
        return jnp.pad(window, pad, constant_values=NEG_INF)

    placements = (
        [strip(7 - rq, 0) for rq in range(NA_QROWS)],
        [strip(3, rq) for rq in range(NA_QROWS)],
        [strip(3 - rq, 4) for rq in range(NA_QROWS)],
    )
    return jnp.stack([jnp.concatenate(strips, axis=2) for strips in placements], axis=1)


def _dil_body(q_ref, k_ref, v_ref, kp_ref, kn_ref, vp_ref, vn_ref, b_ref, o_ref, lse_ref,
              kext_ref, vext_ref, *, dil, tl, planes):
    lse_ref[...] = jnp.zeros_like(lse_ref)

    row0 = pl.program_id(1) * tl
    rows_prompt = SEQ // dil
    len_sample = DEC_SEQ // dil
    in_prompt = row0 < rows_prompt
    pos0 = jnp.where(in_prompt, row0, (row0 - rows_prompt) % len_sample)
    seq_len = jnp.where(in_prompt, rows_prompt, len_sample)

    for p in range(planes):
        for ext_ref, prev_ref, cur_ref, next_ref in ((kext_ref, kp_ref, k_ref, kn_ref),
                                                     (vext_ref, vp_ref, v_ref, vn_ref)):
            ext_ref[p, 0:DIL_RADIUS] = prev_ref[p]
            ext_ref[p, DIL_RADIUS:DIL_RADIUS + tl] = cur_ref[p]
            ext_ref[p, DIL_RADIUS + tl:] = next_ref[p]

        for qs in range(0, tl, DIL_BQ):
            rows = slice(qs, qs + DIL_BQ)
            win = slice(qs, qs + DIL_KW)
            kpos = pos0 + qs - DIL_RADIUS + lax.broadcasted_iota(jnp.int32, (1, DIL_KW), 1)
            edge = jnp.where((kpos >= 0) & (kpos < seq_len), 0.0, NEG_INF).astype(F32)
            for h in range(N_HEADS):
                sl = _head_slice(h)
                s = lax.dot_general(q_ref[p, rows, sl], kext_ref[p, win, sl], NT_DIMS,
                                    preferred_element_type=F32)
                s = s + b_ref[0, h] + edge
                o, m, den = _softmax_pv([s], [vext_ref[p, win, sl]])
                o_ref[p, rows, sl] = o
                lse_ref[p, rows, h:h + 1] = m + jnp.log(den)


def dilated_group_attention(qkv, bias_table, *, group, name):
    d = D_MODEL
    dil = DIL_CONFIGS[group][1]
    assert DIL_CONFIGS[group][0] // (2 * dil) == DIL_RADIUS
    rows = N_TOK // dil
    tl = min(DIL_STEP_ROWS, DEC_SEQ // dil)
    planes = min(dil, DIL_STEP_ROWS // tl)
    assert (SEQ // dil) % tl == 0 and (DEC_SEQ // dil) % tl == 0 and tl % DIL_BQ == 0
    assert dil % planes == 0
    halo_per_tile = tl // DIL_RADIUS
    n_halo = rows // DIL_RADIUS

    def main_spec(part):
        return pl.BlockSpec((planes, tl, d), lambda r, t: (r, t, part))

    def prev_spec(part):
        return pl.BlockSpec((planes, DIL_RADIUS, d),
                            lambda r, t: (r, jnp.maximum(t * halo_per_tile - 1, 0), part))

    def next_spec(part):
        return pl.BlockSpec((planes, DIL_RADIUS, d),
                            lambda r, t: (r, jnp.minimum((t + 1) * halo_per_tile, n_halo - 1), part))

    return pl.pallas_call(
        functools.partial(_dil_body, dil=dil, tl=tl, planes=planes),
        grid=(dil // planes, rows // tl),
        in_specs=[main_spec(0), main_spec(1), main_spec(2),
                  prev_spec(1), next_spec(1), prev_spec(2), next_spec(2),
                  pl.BlockSpec((1, N_HEADS, DIL_BQ, DIL_KW), lambda r, t: (group, 0, 0, 0))],
        out_specs=[pl.BlockSpec((planes, tl, d), lambda r, t: (r, t, 0)),
                   pl.BlockSpec((planes, tl, HEAD_DIM), lambda r, t: (r, t, 0))],
        out_shape=[jax.ShapeDtypeStruct((dil, rows, d), F32),
                   jax.ShapeDtypeStruct((dil, rows, HEAD_DIM), F32)],
        scratch_shapes=[pltpu.VMEM((planes, tl + 2 * DIL_RADIUS, d), BF16),
                        pltpu.VMEM((planes, tl + 2 * DIL_RADIUS, d), BF16)],
        compiler_params=_params("parallel", "parallel"),
        name=name,
    )(*([qkv] * 7), bias_table)


def _t5_bucket(rel):
    nb = T5_BUCKETS // 2
    max_exact = nb // 2
    ret = jnp.where(rel > 0, nb, 0)
    n = jnp.abs(rel)
    n_f = jnp.maximum(n, 1).astype(F32)
    large = max_exact + (jnp.log(n_f / max_exact) / math.log(T5_MAX_DIST / max_exact)
                         * (nb - max_exact)).astype(jnp.int32)
    large = jnp.minimum(large, nb - 1)
    return ret + jnp.where(n < max_exact, n, large)


def dilated_bias_table(t5_table):
    period = DIL_BQ + DIL_KW
    rel = np.arange(-DIL_RADIUS, DIL_RADIUS + 1)
    pad = jnp.full((N_HEADS, period - rel.size), NEG_INF, F32)
    tables = []
    for g, (_, dil) in enumerate(DIL_CONFIGS):
        band = t5_table[:, g].astype(F32)[_t5_bucket(jnp.asarray(rel * dil, jnp.int32))]
        tables.append(_toeplitz(jnp.concatenate([band.T, pad], axis=-1), DIL_BQ, DIL_KW))
    return jnp.stack(tables)


def _merge_project_body(*refs, tm):
    o_refs = refs[:N_GROUPS]
    l_refs = refs[N_GROUPS:2 * N_GROUPS]
    w_ref, r_ref = refs[2 * N_GROUPS:2 * N_GROUPS + 2]
    cross_refs = refs[2 * N_GROUPS + 2:-5]
    out_ref, ltok_ref, otok_ref = refs[-5:-2]
    merged_refs = refs[-2:]
    dils = [dil for _, dil in DIL_CONFIGS]
    assert dils[0] == 1
    i = pl.program_id(0)

    @pl.when(i == 0)
    def _():
        merged_refs[1][...] = jnp.zeros_like(merged_refs[1])

    def to_token_order(dst_ref, idx, src, dil, r):
        if dil == 1:
            dst_ref[idx] = src
        else:
            dst_ref[idx + (pl.ds(r, tm // dil, stride=dil), slice(None))] = src

    def step(prev_ref, cur_ref):
        x = r_ref[...] + jnp.dot(prev_ref[...], w_ref[...], preferred_element_type=F32)
        out_ref[...] = _cross_residual(x, *cross_refs)

        for g, dil in enumerate(dils):
            for r in range(dil):
                to_token_order(ltok_ref, (g,), l_refs[g][r], dil, r)
        lses = [ltok_ref[g] for g in range(N_GROUPS)]
        top = functools.reduce(jnp.maximum, lses)
        wts = [jnp.exp(l - top) for l in lses]
        z = functools.reduce(jnp.add, wts)
        wts = [w / z for w in wts]
        for g, dil in enumerate(dils):
            if dil == 1:
                continue
            for r in range(dil):
                for h in range(N_HEADS):
                    to_token_order(otok_ref, (g - 1, h), o_refs[g][r, :, _head_slice(h)], dil, r)
        for h in range(N_HEADS):
            sl = _head_slice(h)
            parts = [o_refs[0][0, :, sl]] + [otok_ref[g - 1, h] for g in range(1, N_GROUPS)]
            mix = functools.reduce(jnp.add, [p * wts[g][:, h:h + 1] for g, p in enumerate(parts)])
            cur_ref[:, sl] = mix.astype(BF16)

    pl.when(i % 2 == 0)(functools.partial(step, merged_refs[1], merged_refs[0]))
    pl.when(i % 2 == 1)(functools.partial(step, merged_refs[0], merged_refs[1]))


def merge_project_cross_residual(outs, lses, w_stack, layer, res, cross_args, cross_layer, *, tm, name):
    m, d = res.shape
    dils = [dil for _, dil in DIL_CONFIGS]
    assert m % tm == 0 and all(tm % (8 * dil) == 0 for dil in dils)
    n_tiles = m // tm

    def merging(i):
        return jnp.minimum(i, n_tiles - 1)

    def projecting(i):
        return jnp.maximum(i - 1, 0)

    in_specs = [pl.BlockSpec((dil, tm // dil, d), lambda i: (0, merging(i), 0)) for dil in dils]
    in_specs += [pl.BlockSpec((dil, tm // dil, HEAD_DIM), lambda i: (0, merging(i), 0)) for dil in dils]
    in_specs += [_layer_spec((d, d), layer, lambda i: (0, 0), resident=True),
                 pl.BlockSpec((tm, d), lambda i: (projecting(i), 0))]
    in_specs += _cross_specs(tm, cross_layer, tile_of=projecting)
    return pl.pallas_call(
        functools.partial(_merge_project_body, tm=tm),
        grid=(n_tiles + 1,),
        in_specs=in_specs,
        out_specs=pl.BlockSpec((tm, d), lambda i: (projecting(i), 0)),
        out_shape=jax.ShapeDtypeStruct((m, d), F32),
        scratch_shapes=[pltpu.VMEM((N_GROUPS, tm, HEAD_DIM), F32),
                        pltpu.VMEM((N_GROUPS - 1, N_HEADS, tm, HEAD_DIM), F32),
                        pltpu.VMEM((tm, d), BF16),
                        pltpu.VMEM((tm, d), BF16)],
        compiler_params=_params("arbitrary"),
        name=name,
    )(*outs, *lses, w_stack, res, *cross_args)


def _qk_gains(q_gains, k_gains):
    tiled = [jnp.tile(g.astype(F32).reshape(-1, HEAD_DIM), (1, N_HEADS)) for g in (q_gains, k_gains)]
    return jnp.concatenate([tiled[0] * ATTN_SCALE, tiled[1]], axis=1)[:, None]


def kernel(x_prompt, x_sample, mem_prompt, mem_sample, g_mix, g_cross, g_mem, g_mlp, w_qkv_a, q_norm_a, k_norm_a, rpb_a, w_o_a, w_qkv_b, q_norm_b, k_norm_b, t5_table, w_o_b, w_q_x, w_kv_x, q_norm_x, k_norm_x, w_o_x, w_up, w_down):
    d = D_MODEL
    x = (x_prompt.reshape(SEQ, d), x_sample.reshape(DEC_BATCH * DEC_SEQ, d))
    mem = jnp.concatenate([mem_prompt.reshape(N_MEM, d), mem_sample.reshape(DEC_BATCH * N_MEM, d)])
    dil_bias = dilated_bias_table(t5_table)
    na_bias = na_bias_tables(rpb_a)
    (w_qkv_a, w_o_a, w_qkv_b, w_o_b, w_q_x, w_kv_x, w_o_x, w_up, w_down) = (
        w.astype(BF16) for w in (w_qkv_a, w_o_a, w_qkv_b, w_o_b, w_q_x, w_kv_x, w_o_x, w_up, w_down))
    g_mix, g_cross, g_mem, g_mlp = (g.astype(F32)[:, None] for g in (g_mix, g_cross, g_mem, g_mlp))
    qk_gain_a = _qk_gains(q_norm_a, k_norm_a)
    qk_gain_b = _qk_gains(q_norm_b, k_norm_b)
    k_gain_x = jnp.tile(k_norm_x.astype(F32), (1, X_HEADS))[:, None]
    q_gain_x = (q_norm_x.astype(F32) * ATTN_SCALE)[:, None]

    def cross_args(i):
        kv = norm_matmul(mem, g_mem, i, w_kv_x, i, k_gain_x, i,
                         n=2 * X_HEADS * HEAD_DIM, tm=N_SEQS * N_MEM, tn=X_HEADS * HEAD_DIM,
                         name=f"kv_x{i}").reshape(N_SEQS * N_MEM, 2 * X_HEADS * HEAD_DIM)
        return (g_cross, w_q_x, q_gain_x, kv, kv, w_o_x)

    for i in range(DEPTH):
        li = i // 2
        if i % 2 == 0:
            qkv = norm_matmul(x, g_mix, i, w_qkv_a, li, qk_gain_a, li,
                              n=3 * d, tm=QKV_ROWS,
                              tn=QKV_COLS_PAIRED if isinstance(x, tuple) else QKV_COLS,
                              name=f"qkv_a{i}")
            o = neighborhood_attention(qkv.reshape(N_TOK, 3 * d), na_bias, li, name=f"na_attn{i}")
            x = project_cross_residual(o, w_o_a, li, x, cross_args(i), i, tm=PROJ_ROWS,
                                       name=f"wo_cross{i}")
        else:
            outs, lses = [], []
            for g, (_, dil) in enumerate(DIL_CONFIGS):
                qkv = norm_matmul(x, g_mix, i, w_qkv_b, li, qk_gain_b, li * N_GROUPS + g,
                                  n=3 * d, col0=g * 3 * d, tm=QKV_ROWS, tn=QKV_COLS, dil=dil,
                                  name=f"qkv_b{i}_{g}")
                o_g, lse_g = dilated_group_attention(qkv, dil_bias, group=g, name=f"dil_attn{i}_{g}")
                outs.append(o_g)
                lses.append(lse_g)
            x = merge_project_cross_residual(outs, lses, w_o_b, li, x, cross_args(i), i,
                                             tm=MERGE_ROWS, name=f"wo_cross{i}")
        if i < DEPTH - 1:
            x = mlp_residual(x, g_mlp, w_up, w_down, i, tm=MLP_ROWS, tf=MLP_CHUNK, name=f"mlp{i}")

    y_prompt, y_sample = (
        mlp_residual(x, g_mlp, w_up, w_down, DEPTH - 1, tm=MLP_ROWS, tf=MLP_CHUNK, rows=rows,
                     name=f"mlp{DEPTH - 1}_{tag}")
        for tag, rows in (("prompt", (0, SEQ)), ("sample", (SEQ, N_TOK - SEQ))))
    return (y_prompt.reshape(1, SEQ, d), y_sample.reshape(DEC_BATCH, DEC_SEQ, d))
```

```python
import functools
import math

import numpy as np
import jax
import jax.numpy as jnp
from jax import lax
from jax.experimental import pallas as pl
from jax.experimental.pallas import tpu as pltpu

D_MODEL = 2048
SEQ = 8192
DEPTH = 4
DEC_BATCH = 4
DEC_SEQ = 2048
N_TOK = SEQ + DEC_BATCH * DEC_SEQ
N_SEQS = 1 + DEC_BATCH

HEAD_DIM = 128
N_HEADS = D_MODEL // HEAD_DIM
NA_WIN_ROWS = 8
NA_WIN_COLS = 16
GRID_W = 64
DIL_CONFIGS = ((128, 1), (512, 4), (2048, 16))
N_GROUPS = len(DIL_CONFIGS)
DIL_RADIUS = 64
T5_BUCKETS = 32
T5_MAX_DIST = 1024
X_HEADS = 4
N_MEM = 256
D_FF = 4 * D_MODEL
RMS_EPS = 1e-6
ATTN_SCALE = 1.0 / math.sqrt(HEAD_DIM)
NEG_INF = float("-inf")

F32 = jnp.float32
BF16 = jnp.bfloat16
NT_DIMS = (((1,), (1,)), ((), ()))

VMEM_LIMIT_BYTES = 60 * 1024 * 1024

NA_QROWS = 4
NA_BLK = NA_QROWS * GRID_W
NA_KBLKS = 3
NA_NBP = SEQ // NA_BLK
NA_NBS = DEC_SEQ // NA_BLK

DIL_BQ = 128
DIL_KW = DIL_BQ + 2 * DIL_RADIUS
DIL_STEP_ROWS = 512

QKV_ROWS, QKV_COLS = 1024, 2048
QKV_COLS_PAIRED = 1024
PROJ_ROWS = 512
MERGE_ROWS = 256
MLP_ROWS, MLP_CHUNK = 512, 2048

MATMUL_CHUNK = 256
DEINTERLEAVE_STRIDE = 4
SLAB_RING = 4


def _params(*sem):
    return pltpu.CompilerParams(dimension_semantics=sem, vmem_limit_bytes=VMEM_LIMIT_BYTES)


def _layer_spec(block, layer, index_map, resident=False):
    return pl.BlockSpec((None,) + block, lambda *ids: (layer,) + index_map(*ids),
                        pipeline_mode=pl.Buffered(1) if resident else None)


def _group_row_specs(tm, width):
    tp = SEQ // tm
    return tp, [pl.BlockSpec((tm, width), lambda i, *_: (jnp.minimum(i, tp - 1), 0)),
                pl.BlockSpec((tm, width), lambda i, *_: (jnp.maximum(i - tp, 0), 0))]


def _norm_rows(x, g):
    ms = jnp.mean(x * x, axis=-1, keepdims=True)
    return x * lax.rsqrt(ms + RMS_EPS) * g


def _head_slice(h):
    return slice(h * HEAD_DIM, (h + 1) * HEAD_DIM)


def _softmax_pv(logits, values, ones=None):
    m = functools.reduce(jnp.maximum, [jnp.max(s, axis=-1, keepdims=True) for s in logits])
    if ones is None:
        probs = [jnp.exp(s - m) for s in logits]
        den = functools.reduce(jnp.add, [jnp.sum(p, axis=-1, keepdims=True) for p in probs])
        acc = functools.reduce(jnp.add, [jnp.dot(p.astype(BF16), v, preferred_element_type=F32)
                                         for p, v in zip(probs, values)])
        return acc / den, m, den
    res = functools.reduce(jnp.add, [
        jnp.dot(jnp.exp((s - m).astype(BF16)), jnp.concatenate([v, ones], axis=1),
                preferred_element_type=F32)
        for s, v in zip(logits, values)])
    den = res[:, HEAD_DIM:]
    return res[:, :HEAD_DIM] / den, m, den[:, :1]


def _norm_matmul_body(*refs, n_norm_blocks, n_blocks, tm, tn, dil, prompt_tiles):
    n_x = 1 if prompt_tiles is None else 2
    x_refs = refs[:n_x]
    g_ref, w_ref, hg_ref, o_ref, xn_ref = refs[n_x:n_x + 5]
    slab_refs = refs[n_x + 5:]
    i = pl.program_id(0)
    j = pl.program_id(1)

    def normalise(x_ref):
        xn_ref[...] = _norm_rows(x_ref[...], g_ref[...]).astype(BF16)

    if prompt_tiles is None:
        pl.when(j == 0)(functools.partial(normalise, x_refs[0]))
    else:
        pl.when((j == 0) & (i < prompt_tiles))(functools.partial(normalise, x_refs[0]))
        pl.when((j == 0) & (i >= prompt_tiles))(functools.partial(normalise, x_refs[1]))

    def run(head_norm):
        for c in range(tn // MATMUL_CHUNK):
            acc = jnp.dot(xn_ref[...], w_ref[:, c * MATMUL_CHUNK:(c + 1) * MATMUL_CHUNK],
                          preferred_element_type=F32)
            for s in range(MATMUL_CHUNK // HEAD_DIM):
                slab = c * (MATMUL_CHUNK // HEAD_DIM) + s
                sl = _head_slice(slab)
                piece = acc[:, _head_slice(s)]
                if head_norm:
                    piece = _norm_rows(piece, hg_ref[:, sl])
                if dil == 1:
                    o_ref[0, :, sl] = piece.astype(o_ref.dtype)
                    continue
                slab_ref = slab_refs[0]
                slot = slab % SLAB_RING
                slab_ref[slot] = piece
                if dil == DEINTERLEAVE_STRIDE:
                    for r in range(dil):
                        rows = slab_ref[slot, pl.ds(r, tm // dil, stride=dil), :]
                        o_ref[r, :, sl] = rows.astype(o_ref.dtype)
                else:
                    quarter_ref = slab_refs[1]
                    outer = dil // DEINTERLEAVE_STRIDE
                    for b in range(DEINTERLEAVE_STRIDE):
                        quarter_ref[slot, b] = slab_ref[
                            slot, pl.ds(b, tm // DEINTERLEAVE_STRIDE, stride=DEINTERLEAVE_STRIDE), :]
                        for a in range(outer):
                            rows = quarter_ref[slot, b, pl.ds(a, tm // dil, stride=outer), :]
                            o_ref[a * DEINTERLEAVE_STRIDE + b, :, sl] = rows.astype(o_ref.dtype)

    if n_norm_blocks == n_blocks:
        run(True)
    else:
        pl.when(j < n_norm_blocks)(functools.partial(run, True))
        pl.when(j >= n_norm_blocks)(functools.partial(run, False))


def norm_matmul(x, gains, gain_idx, w_stack, layer, head_gains, head_gain_idx, *, n, tm, tn, name,
                col0=0, dil=1):
    pair = isinstance(x, tuple)
    m = sum(part.shape[0] for part in x) if pair else x.shape[0]
    k = w_stack.shape[1]
    n_norm = head_gains.shape[2]
    n_blocks = n // tn
    n_norm_blocks = n_norm // tn
    assert m % tm == 0 and n % tn == 0 and n_norm % tn == 0 and n_norm_blocks >= 1 and col0 % tn == 0
    assert tn % MATMUL_CHUNK == 0 and tm % (16 * dil) == 0
    if pair:
        prompt_tiles, x_specs = _group_row_specs(tm, k)
        x_args = list(x)
    else:
        prompt_tiles, x_specs, x_args = None, [pl.BlockSpec((tm, k), lambda i, j: (i, 0))], [x]
    body = functools.partial(_norm_matmul_body, n_norm_blocks=n_norm_blocks, n_blocks=n_blocks,
                             tm=tm, tn=tn, dil=dil, prompt_tiles=prompt_tiles)
    scratch = [pltpu.VMEM((tm, k), BF16)]
    if dil > 1:
        assert dil % DEINTERLEAVE_STRIDE == 0 and dil // DEINTERLEAVE_STRIDE <= DEINTERLEAVE_STRIDE
        scratch.append(pltpu.VMEM((SLAB_RING, tm, HEAD_DIM), F32))
    if dil > DEINTERLEAVE_STRIDE:
        scratch.append(pltpu.VMEM((SLAB_RING, DEINTERLEAVE_STRIDE, tm // DEINTERLEAVE_STRIDE,
                                   HEAD_DIM), F32))
    return pl.pallas_call(
        body,
        grid=(m // tm, n_blocks),
        in_specs=x_specs + [
            _layer_spec((1, k), gain_idx, lambda i, j: (0, 0)),
            _layer_spec((k, tn), layer, lambda i, j: (0, col0 // tn + j)),
            _layer_spec((1, tn), head_gain_idx, lambda i, j: (0, jnp.minimum(j, n_norm_blocks - 1))),
        ],
        out_specs=pl.BlockSpec((dil, tm // dil, tn), lambda i, j: (0, i, j)),
        out_shape=jax.ShapeDtypeStruct((dil, m // dil, n), BF16),
        scratch_shapes=scratch,
        compiler_params=_params("arbitrary" if pair else "parallel", "arbitrary"),
        name=name,
    )(*x_args, gains, w_stack, head_gains)


def memory_kv(mem, gains, w_stack, head_gains, *, name):
    m, k = mem.shape
    n_layers, _, n = w_stack.shape
    tn = head_gains.shape[2]
    assert n == 2 * tn and tn % MATMUL_CHUNK == 0
    body = functools.partial(_norm_matmul_body, n_norm_blocks=1, n_blocks=2, tm=m, tn=tn, dil=1,
                             prompt_tiles=None)
    return pl.pallas_call(
        body,
        grid=(n_layers, 2),
        in_specs=[
            pl.BlockSpec((m, k), lambda l, j: (0, 0)),
            pl.BlockSpec((None, 1, k), lambda l, j: (l, 0, 0)),
            pl.BlockSpec((None, k, tn), lambda l, j: (l, 0, j)),
            pl.BlockSpec((None, 1, tn), lambda l, j: (l, 0, 0)),
        ],
        out_specs=pl.BlockSpec((1, m, tn), lambda l, j: (l, 0, j)),
        out_shape=jax.ShapeDtypeStruct((n_layers, m, n), BF16),
        scratch_shapes=[pltpu.VMEM((m, k), BF16)],
        compiler_params=_params("arbitrary", "arbitrary"),
        name=name,
    )(mem, gains, w_stack, head_gains)


def _project_cross_body(a_ref, w_ref, *refs, prompt_tiles):
    o_ref = refs[-1]
    if prompt_tiles is None:
        res, cross_refs = refs[0][...], refs[1:-1]
    else:
        res = jnp.where(pl.program_id(0) < prompt_tiles, refs[0][...], refs[1][...])
        cross_refs = refs[2:-1]
    x = res + jnp.dot(a_ref[...], w_ref[...], preferred_element_type=F32)
    o_ref[...] = _cross_residual(x, *cross_refs)


def project_cross_residual(a, w_stack, layer, res, cross_args, cross_layer, *, tm, name):
    m, k = a.shape
    n = w_stack.shape[2]
    assert m % tm == 0
    if isinstance(res, tuple):
        prompt_tiles, res_specs = _group_row_specs(tm, n)
        res_args = list(res)
    else:
        prompt_tiles, res_specs, res_args = None, [pl.BlockSpec((tm, n), lambda i: (i, 0))], [res]
    return pl.pallas_call(
        functools.partial(_project_cross_body, prompt_tiles=prompt_tiles),
        grid=(m // tm,),
        in_specs=[pl.BlockSpec((tm, k), lambda i: (i, 0)),
                  _layer_spec((k, n), layer, lambda i: (0, 0), resident=True)] + res_specs
        + _cross_specs(tm, cross_layer),
        out_specs=pl.BlockSpec((tm, n), lambda i: (i, 0)),
        out_shape=jax.ShapeDtypeStruct((m, n), F32),
        compiler_params=_params("arbitrary" if prompt_tiles is not None else "parallel"),
        name=name,
    )(a, w_stack, *res_args, *cross_args)


def _mlp_body(x_ref, g_ref, wu_ref, wd_ref, o_ref, xn_ref):
    @pl.when(pl.program_id(1) == 0)
    def _():
        x = x_ref[...]
        xn_ref[...] = _norm_rows(x, g_ref[...]).astype(BF16)
        o_ref[...] = x

    h = jnp.dot(xn_ref[...], wu_ref[...], preferred_element_type=F32)
    h = jnp.maximum(h, 0.0)
    o_ref[...] += jnp.dot((h * h).astype(BF16), wd_ref[...], preferred_element_type=F32)


def mlp_residual(x, gains, w_up, w_down, layer, *, tm, tf, name, rows=None):
    first, count = (0, x.shape[0]) if rows is None else rows
    d = x.shape[1]
    ff = w_up.shape[2]
    assert first % tm == 0 and count % tm == 0 and ff % tf == 0
    return pl.pallas_call(
        _mlp_body,
        grid=(count // tm, ff // tf),
        in_specs=[
            pl.BlockSpec((tm, d), lambda i, f: (first // tm + i, 0)),
            _layer_spec((1, d), layer, lambda i, f: (0, 0)),
            _layer_spec((d, tf), layer, lambda i, f: (0, f)),
            _layer_spec((tf, d), layer, lambda i, f: (f, 0)),
        ],
        out_specs=pl.BlockSpec((tm, d), lambda i, f: (i, 0)),
        out_shape=jax.ShapeDtypeStruct((count, d), F32),
        scratch_shapes=[pltpu.VMEM((tm, d), BF16)],
        compiler_params=_params("parallel", "arbitrary"),
        name=name,
    )(x, gains, w_up, w_down)


def _cross_residual(x, g_ref, wq_ref, qn_ref, k_ref, v_ref, wo_ref):
    xn = _norm_rows(x, g_ref[...]).astype(BF16)
    q = jnp.dot(xn, wq_ref[...], preferred_element_type=F32)
    heads = []
    for h in range(X_HEADS):
        sl = _head_slice(h)
        qh = _norm_rows(q[:, sl], qn_ref[...]).astype(BF16)
        s = lax.dot_general(qh, k_ref[:, sl], NT_DIMS, preferred_element_type=F32)
        oh, _, _ = _softmax_pv([s], [v_ref[:, sl]])
        heads.append(oh.astype(BF16))
    o = jnp.concatenate(heads, axis=-1)
    return x + jnp.dot(o, wo_ref[...], preferred_element_type=F32)


def _cross_specs(tm, layer, tile_of=lambda i: i):
    d = D_MODEL
    dx = X_HEADS * HEAD_DIM
    assert SEQ % tm == 0 and DEC_SEQ % tm == 0
    tiles_p = SEQ // tm
    tiles_s = DEC_SEQ // tm

    def mem_seq(i):
        return jnp.where(i < tiles_p, 0, 1 + (i - tiles_p) // tiles_s)

    return [
        _layer_spec((1, d), layer, lambda i: (0, 0)),
        _layer_spec((d, dx), layer, lambda i: (0, 0), resident=True),
        _layer_spec((1, HEAD_DIM), layer, lambda i: (0, 0)),
        _layer_spec((N_MEM, dx), layer, lambda i: (mem_seq(tile_of(i)), 0)),
        _layer_spec((N_MEM, dx), layer, lambda i: (mem_seq(tile_of(i)), 1)),
        _layer_spec((dx, d), layer, lambda i: (0, 0), resident=True),
    ]


def _na_block_info(i):
    in_prompt = i < NA_NBP
    local = jnp.where(in_prompt, i, (i - NA_NBP) % NA_NBS)
    nblk = jnp.where(in_prompt, NA_NBP, NA_NBS)
    base = jnp.where(in_prompt, 0, NA_NBP + ((i - NA_NBP) // NA_NBS) * NA_NBS)
    kblk = base + jnp.clip(local - 1, 0, nblk - NA_KBLKS)
    placement = jnp.where(local == 0, 0, jnp.where(local == nblk - 1, 2, 1))
    return kblk, placement


def _na_body(q_ref, k0_ref, k1_ref, k2_ref, v0_ref, v1_ref, v2_ref, b_ref, o_ref):
    k_refs = (k0_ref, k1_ref, k2_ref)
    v_refs = (v0_ref, v1_ref, v2_ref)
    ones = jnp.ones((NA_BLK, HEAD_DIM), BF16)
    for h in range(N_HEADS):
        sl = _head_slice(h)
        q = q_ref[:, sl]
        logits = []
        for c in range(NA_KBLKS):
            s = lax.dot_general(q, k_refs[c][:, sl], NT_DIMS, preferred_element_type=F32)
            logits.append(s + b_ref[0, h, :, c * NA_BLK:(c + 1) * NA_BLK])
        o, _, _ = _softmax_pv(logits, [v_ref[:, sl] for v_ref in v_refs], ones)
        o_ref[:, sl] = o.astype(o_ref.dtype)


def neighborhood_attention(qkv, bias_tables, layer, *, name):
    d = D_MODEL

    def kv_spec(c, part):
        return pl.BlockSpec((NA_BLK, d), lambda i: (_na_block_info(i)[0] + c, part))

    return pl.pallas_call(
        _na_body,
        grid=(N_TOK // NA_BLK,),
        in_specs=[pl.BlockSpec((NA_BLK, d), lambda i: (i, 0))]
        + [kv_spec(c, 1) for c in range(NA_KBLKS)]
        + [kv_spec(c, 2) for c in range(NA_KBLKS)]
        + [pl.BlockSpec((None, 1, N_HEADS, NA_BLK, NA_KBLKS * NA_BLK),
                        lambda i: (layer, _na_block_info(i)[1], 0, 0, 0))],
        out_specs=pl.BlockSpec((NA_BLK, d), lambda i: (i, 0)),
        out_shape=jax.ShapeDtypeStruct((N_TOK, d), BF16),
        compiler_params=_params("parallel"),
        name=name,
    )(*([qkv] * (1 + 2 * NA_KBLKS)), bias_tables)


def _toeplitz(v, n_rows, n_cols):
    period = v.shape[-1]
    assert period >= n_rows + n_cols - 1
    flat = jnp.tile(v, (1,) * (v.ndim - 1) + (n_rows,))[..., :n_rows * (period - 1)]
    return flat.reshape(v.shape[:-1] + (n_rows, period - 1))[..., :n_cols]


def na_bias_tables(rpb):
    n_layers = rpb.shape[0]
    n_krows = NA_KBLKS * NA_QROWS
    n_dr = 2 * NA_WIN_ROWS - 1
    assert n_krows - NA_WIN_ROWS == NA_QROWS
    c = np.arange(GRID_W)
    cs = np.clip(c - NA_WIN_COLS // 2, 0, GRID_W - NA_WIN_COLS)
    col_ok = (c[None, :] >= cs[:, None]) & (c[None, :] < cs[:, None] + NA_WIN_COLS)
    dist = np.arange(2 * GRID_W)
    dist = np.where(dist < GRID_W, dist, dist - 2 * GRID_W)
    by_dist = rpb.astype(F32)[..., np.clip(dist, -(NA_WIN_COLS - 1), NA_WIN_COLS - 1) + (NA_WIN_COLS - 1)]
    tiles = jnp.where(col_ok, _toeplitz(by_dist, GRID_W, GRID_W), NEG_INF)
    side_by_side = tiles.transpose(0, 1, 3, 2, 4).reshape(n_layers, N_HEADS, GRID_W, n_dr * GRID_W)

    def strip(first_dr, lead):
        window = side_by_side[..., first_dr * GRID_W:(first_dr + NA_WIN_ROWS) * GRID_W]
        pad = [(0, 0)] * 3 + [(lead * GRID_W, (n_krows - NA_WIN_ROWS - lead) * GRID_W)]
        return jnp.pad(window, pad, constant_values=NEG_INF)

    placements = (
        [strip(7 - rq, 0) for rq in range(NA_QROWS)],
        [strip(3, rq) for rq in range(NA_QROWS)],
        [strip(3 - rq, 4) for rq in range(NA_QROWS)],
    )
    return jnp.stack([jnp.concatenate(strips, axis=2) for strips in placements], axis=1)


def _dil_body(q_ref, k_ref, v_ref, kp_ref, kn_ref, vp_ref, vn_ref, b_ref, o_ref, lse_ref,
              kext_ref, vext_ref, *, dil, tl, planes):
    lse_ref[...] = jnp.zeros_like(lse_ref)

    row0 = pl.program_id(1) * tl
    rows_prompt = SEQ // dil
    len_sample = DEC_SEQ // dil
    in_prompt = row0 < rows_prompt
    pos0 = jnp.where(in_prompt, row0, (row0 - rows_prompt) % len_sample)
    seq_len = jnp.where(in_prompt, rows_prompt, len_sample)

    for p in range(planes):
        for ext_ref, prev_ref, cur_ref, next_ref in ((kext_ref, kp_ref, k_ref, kn_ref),
                                                     (vext_ref, vp_ref, v_ref, vn_ref)):
            ext_ref[p, 0:DIL_RADIUS] = prev_ref[p]
            ext_ref[p, DIL_RADIUS:DIL_RADIUS + tl] = cur_ref[p]
            ext_ref[p, DIL_RADIUS + tl:] = next_ref[p]

        for qs in range(0, tl, DIL_BQ):
            rows = slice(qs, qs + DIL_BQ)
            win = slice(qs, qs + DIL_KW)
            kpos = pos0 + qs - DIL_RADIUS + lax.broadcasted_iota(jnp.int32, (1, DIL_KW), 1)
            edge = jnp.where((kpos >= 0) & (kpos < seq_len), 0.0, NEG_INF).astype(F32)
            for h in range(N_HEADS):
                sl = _head_slice(h)
                s = lax.dot_general(q_ref[p, rows, sl], kext_ref[p, win, sl], NT_DIMS,
                                    preferred_element_type=F32)
                s = s + b_ref[0, h] + edge
                o, m, den = _softmax_pv([s], [vext_ref[p, win, sl]])
                o_ref[p, rows, sl] = o
                lse_ref[p, rows, h:h + 1] = m + jnp.log(den)


def dilated_group_attention(qkv, bias_table, *, group, name):
    d = D_MODEL
    dil = DIL_CONFIGS[group][1]
    assert DIL_CONFIGS[group][0] // (2 * dil) == DIL_RADIUS
    rows = N_TOK // dil
    tl = min(DIL_STEP_ROWS, DEC_SEQ // dil)
    planes = min(dil, DIL_STEP_ROWS // tl)
    assert (SEQ // dil) % tl == 0 and (DEC_SEQ // dil) % tl == 0 and tl % DIL_BQ == 0
    assert dil % planes == 0
    halo_per_tile = tl // DIL_RADIUS
    n_halo = rows // DIL_RADIUS

    def main_spec(part):
        return pl.BlockSpec((planes, tl, d), lambda r, t: (r, t, part))

    def prev_spec(part):
        return pl.BlockSpec((planes, DIL_RADIUS, d),
                            lambda r, t: (r, jnp.maximum(t * halo_per_tile - 1, 0), part))

    def next_spec(part):
        return pl.BlockSpec((planes, DIL_RADIUS, d),
                            lambda r, t: (r, jnp.minimum((t + 1) * halo_per_tile, n_halo - 1), part))

    return pl.pallas_call(
        functools.partial(_dil_body, dil=dil, tl=tl, planes=planes),
        grid=(dil // planes, rows // tl),
        in_specs=[main_spec(0), main_spec(1), main_spec(2),
                  prev_spec(1), next_spec(1), prev_spec(2), next_spec(2),
                  pl.BlockSpec((1, N_HEADS, DIL_BQ, DIL_KW), lambda r, t: (group, 0, 0, 0))],
        out_specs=[pl.BlockSpec((planes, tl, d), lambda r, t: (r, t, 0)),
                   pl.BlockSpec((planes, tl, HEAD_DIM), lambda r, t: (r, t, 0))],
        out_shape=[jax.ShapeDtypeStruct((dil, rows, d), F32),
                   jax.ShapeDtypeStruct((dil, rows, HEAD_DIM), F32)],
        scratch_shapes=[pltpu.VMEM((planes, tl + 2 * DIL_RADIUS, d), BF16),
                        pltpu.VMEM((planes, tl + 2 * DIL_RADIUS, d), BF16)],
        compiler_params=_params("parallel", "parallel"),
        name=name,
    )(*([qkv] * 7), bias_table)


def _t5_bucket(rel):
    nb = T5_BUCKETS // 2
    max_exact = nb // 2
    ret = jnp.where(rel > 0, nb, 0)
    n = jnp.abs(rel)
    n_f = jnp.maximum(n, 1).astype(F32)
    large = max_exact + (jnp.log(n_f / max_exact) / math.log(T5_MAX_DIST / max_exact)
                         * (nb - max_exact)).astype(jnp.int32)
    large = jnp.minimum(large, nb - 1)
    return ret + jnp.where(n < max_exact, n, large)


def dilated_bias_table(t5_table):
    period = DIL_BQ + DIL_KW
    rel = np.arange(-DIL_RADIUS, DIL_RADIUS + 1)
    pad = jnp.full((N_HEADS, period - rel.size), NEG_INF, F32)
    tables = []
    for g, (_, dil) in enumerate(DIL_CONFIGS):
        band = t5_table[:, g].astype(F32)[_t5_bucket(jnp.asarray(rel * dil, jnp.int32))]
        tables.append(_toeplitz(jnp.concatenate([band.T, pad], axis=-1), DIL_BQ, DIL_KW))
    return jnp.stack(tables)


def _merge_project_body(*refs, tm):
    o_refs = refs[:N_GROUPS]
    l_refs = refs[N_GROUPS:2 * N_GROUPS]
    w_ref, r_ref = refs[2 * N_GROUPS:2 * N_GROUPS + 2]
    cross_refs = refs[2 * N_GROUPS + 2:-5]
    out_ref, ltok_ref, otok_ref = refs[-5:-2]
    merged_refs = refs[-2:]
    dils = [dil for _, dil in DIL_CONFIGS]
    assert dils[0] == 1
    i = pl.program_id(0)

    @pl.when(i == 0)
    def _():
        merged_refs[1][...] = jnp.zeros_like(merged_refs[1])

    def to_token_order(dst_ref, idx, src, dil, r):
        if dil == 1:
            dst_ref[idx] = src
        else:
            dst_ref[idx + (pl.ds(r, tm // dil, stride=dil), slice(None))] = src

    def step(prev_ref, cur_ref):
        x = r_ref[...] + jnp.dot(prev_ref[...], w_ref[...], preferred_element_type=F32)
        out_ref[...] = _cross_residual(x, *cross_refs)

        for g, dil in enumerate(dils):
            for r in range(dil):
                to_token_order(ltok_ref, (g,), l_refs[g][r], dil, r)
        lses = [ltok_ref[g] for g in range(N_GROUPS)]
        top = functools.reduce(jnp.maximum, lses)
        wts = [jnp.exp(l - top) for l in lses]
        z = functools.reduce(jnp.add, wts)
        wts = [w / z for w in wts]
        for g, dil in enumerate(dils):
            if dil == 1:
                continue
            for r in range(dil):
                for h in range(N_HEADS):
                    to_token_order(otok_ref, (g - 1, h), o_refs[g][r, :, _head_slice(h)], dil, r)
        for h in range(N_HEADS):
            sl = _head_slice(h)
            parts = [o_refs[0][0, :, sl]] + [otok_ref[g - 1, h] for g in range(1, N_GROUPS)]
            mix = functools.reduce(jnp.add, [p * wts[g][:, h:h + 1] for g, p in enumerate(parts)])
            cur_ref[:, sl] = mix.astype(BF16)

    pl.when(i % 2 == 0)(functools.partial(step, merged_refs[1], merged_refs[0]))
    pl.when(i % 2 == 1)(functools.partial(step, merged_refs[0], merged_refs[1]))


def merge_project_cross_residual(outs, lses, w_stack, layer, res, cross_args, cross_layer, *, tm, name):
    m, d = res.shape
    dils = [dil for _, dil in DIL_CONFIGS]
    assert m % tm == 0 and all(tm % (8 * dil) == 0 for dil in dils)
    n_tiles = m // tm

    def merging(i):
        return jnp.minimum(i, n_tiles - 1)

    def projecting(i):
        return jnp.maximum(i - 1, 0)

    in_specs = [pl.BlockSpec((dil, tm // dil, d), lambda i: (0, merging(i), 0)) for dil in dils]
    in_specs += [pl.BlockSpec((dil, tm // dil, HEAD_DIM), lambda i: (0, merging(i), 0)) for dil in dils]
    in_specs += [_layer_spec((d, d), layer, lambda i: (0, 0), resident=True),
                 pl.BlockSpec((tm, d), lambda i: (projecting(i), 0))]
    in_specs += _cross_specs(tm, cross_layer, tile_of=projecting)
    return pl.pallas_call(
        functools.partial(_merge_project_body, tm=tm),
        grid=(n_tiles + 1,),
        in_specs=in_specs,
        out_specs=pl.BlockSpec((tm, d), lambda i: (projecting(i), 0)),
        out_shape=jax.ShapeDtypeStruct((m, d), F32),
        scratch_shapes=[pltpu.VMEM((N_GROUPS, tm, HEAD_DIM), F32),
                        pltpu.VMEM((N_GROUPS - 1, N_HEADS, tm, HEAD_DIM), F32),
                        pltpu.VMEM((tm, d), BF16),
                        pltpu.VMEM((tm, d), BF16)],
        compiler_params=_params("arbitrary"),
        name=name,
    )(*outs, *lses, w_stack, res, *cross_args)


def _qk_gains(q_gains, k_gains):
    tiled = [jnp.tile(g.astype(F32).reshape(-1, HEAD_DIM), (1, N_HEADS)) for g in (q_gains, k_gains)]
    return jnp.concatenate([tiled[0] * ATTN_SCALE, tiled[1]], axis=1)[:, None]


def kernel(x_prompt, x_sample, mem_prompt, mem_sample, g_mix, g_cross, g_mem, g_mlp, w_qkv_a, q_norm_a, k_norm_a, rpb_a, w_o_a, w_qkv_b, q_norm_b, k_norm_b, t5_table, w_o_b, w_q_x, w_kv_x, q_norm_x, k_norm_x, w_o_x, w_up, w_down):
    d = D_MODEL
    x = (x_prompt.reshape(SEQ, d), x_sample.reshape(DEC_BATCH * DEC_SEQ, d))
    mem = jnp.concatenate([mem_prompt.reshape(N_MEM, d), mem_sample.reshape(DEC_BATCH * N_MEM, d)])
    dil_bias = dilated_bias_table(t5_table)
    na_bias = na_bias_tables(rpb_a)
    (w_qkv_a, w_o_a, w_qkv_b, w_o_b, w_q_x, w_kv_x, w_o_x, w_up, w_down) = (
        w.astype(BF16) for w in (w_qkv_a, w_o_a, w_qkv_b, w_o_b, w_q_x, w_kv_x, w_o_x, w_up, w_down))
    g_mix, g_cross, g_mem, g_mlp = (g.astype(F32)[:, None] for g in (g_mix, g_cross, g_mem, g_mlp))
    qk_gain_a = _qk_gains(q_norm_a, k_norm_a)
    qk_gain_b = _qk_gains(q_norm_b, k_norm_b)
    k_gain_x = jnp.tile(k_norm_x.astype(F32), (1, X_HEADS))[:, None]
    q_gain_x = (q_norm_x.astype(F32) * ATTN_SCALE)[:, None]

    kv = memory_kv(mem, g_mem, w_kv_x, k_gain_x, name="kv_x")

    def cross_args(i):
        return (g_cross, w_q_x, q_gain_x, kv, kv, w_o_x)

    for i in range(DEPTH):
        li = i // 2
        if i % 2 == 0:
            qkv = norm_matmul(x, g_mix, i, w_qkv_a, li, qk_gain_a, li,
                              n=3 * d, tm=QKV_ROWS,
                              tn=QKV_COLS_PAIRED if isinstance(x, tuple) else QKV_COLS,
                              name=f"qkv_a{i}")
            o = neighborhood_attention(qkv.reshape(N_TOK, 3 * d), na_bias, li, name=f"na_attn{i}")
            x = project_cross_residual(o, w_o_a, li, x, cross_args(i), i, tm=PROJ_ROWS,
                                       name=f"wo_cross{i}")
        else:
            outs, lses = [], []
            for g, (_, dil) in enumerate(DIL_CONFIGS):
                qkv = norm_matmul(x, g_mix, i, w_qkv_b, li, qk_gain_b, li * N_GROUPS + g,
                                  n=3 * d, col0=g * 3 * d, tm=QKV_ROWS, tn=QKV_COLS, dil=dil,
                                  name=f"qkv_b{i}_{g}")
                o_g, lse_g = dilated_group_attention(qkv, dil_bias, group=g, name=f"dil_attn{i}_{g}")
                outs.append(o_g)
                lses.append(lse_g)
            x = merge_project_cross_residual(outs, lses, w_o_b, li, x, cross_args(i), i,
                                             tm=MERGE_ROWS, name=f"wo_cross{i}")
        if i < DEPTH - 1:
            x = mlp_residual(x, g_mlp, w_up, w_down, i, tm=MLP_ROWS, tf=MLP_CHUNK, name=f"mlp{i}")

    y_prompt, y_sample = (
        mlp_residual(x, g_mlp, w_up, w_down, DEPTH - 1, tm=MLP_ROWS, tf=MLP_CHUNK, rows=rows,
                     name=f"mlp{DEPTH - 1}_{tag}")
        for tag, rows in (("prompt", (0, SEQ)), ("sample", (SEQ, N_TOK - SEQ))))
    return (y_prompt.reshape(1, SEQ, d), y_sample.reshape(DEC_BATCH, DEC_SEQ, d))
```
